```python
import jax, jax.numpy as jnp
from jax import lax
import numpy as np

D_MODEL = 2048
BATCH = 1
SEQ = 16384
DEPTH = 2

N_HEADS = 16
HEAD_DIM = D_MODEL // N_HEADS
ROPE_DIM = HEAD_DIM // 4
ROPE_THETA = 500000.0
DIL_BRANCHES = ((128, 1), (512, 4), (2048, 16))
ATT_BLOCK = 128
CONV_WIDTH = 3
CONV_DIM = D_MODEL // 2
POOL_DIM = D_MODEL // 2
POOL_WINDOWS = (2, 4, 8, 16)
POOL_GROUPS = len(POOL_WINDOWS)
POOL_GROUP = POOL_DIM // POOL_GROUPS
MIX_IN = 3 * CONV_DIM + POOL_DIM
D_FF = 256 * ((8 * D_MODEL // 3 + 255) // 256)
N_EVEN = (DEPTH + 1) // 2
N_ODD = DEPTH // 2
RMS_EPS = 1e-6
NEG_INF = -1e30

kernel_name = "hybrid_conv_pool_dilated_attn_macaron"


def rms_norm(x, g):
    xf = x.astype(jnp.float32)
    y = xf * lax.rsqrt(jnp.mean(xf * xf, axis=-1, keepdims=True) + RMS_EPS)
    return (y * g.astype(jnp.float32)).astype(x.dtype)


def swiglu(x, w1, w3, w2):
    return (jax.nn.silu(x @ w1) * (x @ w3)) @ w2


def rope_tables(positions):
    inv = ROPE_THETA ** (-jnp.arange(0, ROPE_DIM, 2, dtype=jnp.float32) / ROPE_DIM)
    ang = positions.astype(jnp.float32)[..., None] * inv
    return jnp.cos(ang)[:, :, None, :], jnp.sin(ang)[:, :, None, :]


def apply_rope(t, cos, sin):
    tr = t[..., :ROPE_DIM].astype(jnp.float32)
    x1, x2 = tr[..., :ROPE_DIM // 2], tr[..., ROPE_DIM // 2:]
    rot = jnp.concatenate([x1 * cos - x2 * sin, x2 * cos + x1 * sin], axis=-1)
    return jnp.concatenate([rot.astype(t.dtype), t[..., ROPE_DIM:]], axis=-1)


def pool_mixer(u, pool_w, pool_scale):
    B, S, _ = u.shape
    ug = u.reshape(B, S, POOL_GROUPS, POOL_GROUP).astype(jnp.float32)
    cs = jnp.cumsum(ug, axis=1)
    t = jnp.arange(S)
    outs = []
    for g, w in enumerate(POOL_WINDOWS):
        c = cs[:, :, g]
        lag = jnp.pad(c, ((0, 0), (w, 0), (0, 0)))[:, :S]
        cnt = jnp.minimum(t + 1, w).astype(jnp.float32)[None, :, None]
        outs.append((c - lag) / cnt - ug[:, :, g])
    pooled = jnp.stack(outs, axis=2).astype(u.dtype)
    mixed = jnp.einsum('bsgc,gcd->bsgd', pooled, pool_w)
    return mixed.reshape(B, S, POOL_DIM) * pool_scale


def conv_pool_mixer(h, w_in, conv_w, pool_w, pool_scale, w_out):
    z = h @ w_in
    gate_b, gate_c, hv, u = jnp.split(z, [CONV_DIM, 2 * CONV_DIM, 3 * CONV_DIM], axis=-1)
    cx = gate_c * hv
    conv = lax.conv_general_dilated(
        cx, conv_w[:, None, :].astype(cx.dtype), window_strides=(1,),
        padding=[(CONV_WIDTH - 1, 0)], dimension_numbers=('NWC', 'WIO', 'NWC'),
        feature_group_count=CONV_DIM)
    y_a = gate_b * conv
    y_b = pool_mixer(u, pool_w, pool_scale)
    return jnp.concatenate([y_a, y_b], axis=-1) @ w_out


def dilated_branch(q, k, v, dil, span):
    B, S, H, E = q.shape
    L = S // dil
    Lp = -(-L // ATT_BLOCK) * ATT_BLOCK
    nb = Lp // ATT_BLOCK

    def to_blocks(t):
        t = t.reshape(B, L, dil, H, E).transpose(0, 2, 1, 3, 4)
        t = jnp.pad(t, ((0, 0), (0, 0), (0, Lp - L), (0, 0), (0, 0)))
        return t.reshape(B, dil, nb, ATT_BLOCK, H, E)

    def with_prev(t):
        prev = jnp.pad(t, ((0, 0), (0, 0), (1, 0), (0, 0), (0, 0), (0, 0)))[:, :, :-1]
        return jnp.concatenate([prev, t], axis=3)

    qb = to_blocks(q)
    kk = with_prev(to_blocks(k))
    vv = with_prev(to_blocks(v))
    s = jnp.einsum('brnqhe,brnkhe->brnhqk', qb, kk,
                   preferred_element_type=jnp.float32) * (1.0 / np.sqrt(HEAD_DIM))
    a = jnp.arange(ATT_BLOCK)[None, :, None]
    c = jnp.arange(2 * ATT_BLOCK)[None, None, :]
    blk = jnp.arange(nb)[:, None, None]
    dist = a + ATT_BLOCK - c
    valid = (dist >= 0) & (dist <= span) & ((blk > 0) | (c >= ATT_BLOCK))
    s = jnp.where(valid[None, None, :, None], s, NEG_INF)
    m = jnp.max(s, axis=-1, keepdims=True)
    p = jnp.exp(s - m)
    l = jnp.sum(p, axis=-1, keepdims=True)
    o = jnp.einsum('brnhqk,brnkhe->brnqhe', p, vv.astype(jnp.float32))
    o = o / jnp.transpose(l[..., 0], (0, 1, 2, 4, 3))[..., None]
    lse = jnp.transpose((m + jnp.log(l))[..., 0], (0, 1, 2, 4, 3))
    o = o.reshape(B, dil, Lp, H, E)[:, :, :L].transpose(0, 2, 1, 3, 4).reshape(B, S, H, E)
    lse = lse.reshape(B, dil, Lp, H)[:, :, :L].transpose(0, 2, 1, 3).reshape(B, S, H)
    return o, lse


def dilated_attention(h, w_qkv, w_o, cos, sin):
    B, S, _ = h.shape
    qkv = (h @ w_qkv).reshape(B, S, 3, N_HEADS, HEAD_DIM)
    q = apply_rope(qkv[:, :, 0], cos, sin)
    k = apply_rope(qkv[:, :, 1], cos, sin)
    v = qkv[:, :, 2]
    outs, lses = [], []
    for window, dil in DIL_BRANCHES:
        o, lse = dilated_branch(q, k, v, dil, window // dil)
        outs.append(o)
        lses.append(lse)
    wts = jax.nn.softmax(jnp.stack(lses, axis=0), axis=0)
    o = jnp.einsum('gbsh,gbshe->bshe', wts, jnp.stack(outs, axis=0))
    return o.astype(h.dtype).reshape(B, S, D_MODEL) @ w_o


def setup_inputs(seed: int = 0) -> dict:
    key = jax.random.key(seed)
    ks = jax.random.split(key, 16)
    f32 = jnp.float32
    nrm = lambda k, shape, scale: jax.random.normal(k, shape, f32) * scale
    x = jax.random.normal(ks[0], (BATCH, SEQ, D_MODEL), f32)
    offset = jax.random.randint(ks[1], (BATCH, 1), 0, 4096, dtype=jnp.int32)
    positions = offset + jnp.arange(SEQ, dtype=jnp.int32)[None, :]
    return {
        "x": x,
        "positions": positions,
        "norm_g": 1.0 + nrm(ks[2], (DEPTH, 3, D_MODEL), 0.02),
        "ffn_w1": nrm(ks[3], (DEPTH, 2, D_MODEL, D_FF), D_MODEL ** -0.5),
        "ffn_w3": nrm(ks[4], (DEPTH, 2, D_MODEL, D_FF), D_MODEL ** -0.5),
        "ffn_w2": nrm(ks[5], (DEPTH, 2, D_FF, D_MODEL), D_FF ** -0.5),
        "mix_w_in": nrm(ks[6], (N_EVEN, D_MODEL, MIX_IN), D_MODEL ** -0.5),
        "conv_w": nrm(ks[7], (N_EVEN, CONV_WIDTH, CONV_DIM), CONV_WIDTH ** -0.5),
        "pool_w": nrm(ks[8], (N_EVEN, POOL_GROUPS, POOL_GROUP, POOL_GROUP), POOL_GROUP ** -0.5),
        "pool_scale": 1.0 + nrm(ks[9], (N_EVEN, POOL_DIM), 0.02),
        "mix_w_out": nrm(ks[10], (N_EVEN, D_MODEL, D_MODEL), D_MODEL ** -0.5),
        "attn_w_qkv": nrm(ks[11], (N_ODD, D_MODEL, 3 * D_MODEL), D_MODEL ** -0.5),
        "attn_w_o": nrm(ks[12], (N_ODD, D_MODEL, D_MODEL), D_MODEL ** -0.5),
        "final_g": 1.0 + nrm(ks[13], (D_MODEL,), 0.02),
    }


def reference(x, positions, norm_g, ffn_w1, ffn_w3, ffn_w2, mix_w_in, conv_w, pool_w,
              pool_scale, mix_w_out, attn_w_qkv, attn_w_o, final_g):
    cos, sin = rope_tables(positions)
    h = x
    for i in range(DEPTH):
        h = h + 0.5 * swiglu(rms_norm(h, norm_g[i, 0]), ffn_w1[i, 0], ffn_w3[i, 0], ffn_w2[i, 0])
        hn = rms_norm(h, norm_g[i, 1])
        j = i // 2
        if i % 2 == 0:
            h = h + conv_pool_mixer(hn, mix_w_in[j], conv_w[j], pool_w[j], pool_scale[j], mix_w_out[j])
        else:
            h = h + dilated_attention(hn, attn_w_qkv[j], attn_w_o[j], cos, sin)
        h = h + 0.5 * swiglu(rms_norm(h, norm_g[i, 2]), ffn_w1[i, 1], ffn_w3[i, 1], ffn_w2[i, 1])
    return rms_norm(h, final_g)
```

```python
import functools

import numpy as np
import jax
import jax.numpy as jnp
from jax import lax
from jax.experimental import pallas as pl
from jax.experimental.pallas import tpu as pltpu

F32 = jnp.float32
BF16 = jnp.bfloat16

HEAD_DIM = 128
ROPE_DIM = HEAD_DIM // 4
ROPE_THETA = 500000.0
DIL_BRANCHES = ((128, 1), (512, 4), (2048, 16))
ATT_BLOCK = 128
CONV_WIDTH = 3
POOL_WINDOWS = (2, 4, 8, 16)
HALO = 16
LSE_LANES = 128
RMS_EPS = 1e-6
NEG_INF = -1e30

V7X_VMEM_BYTES = 64 * 1024 * 1024
VMEM_LIMIT = V7X_VMEM_BYTES - 6 * 1024 * 1024


def _params(sem):
    return pltpu.CompilerParams(dimension_semantics=sem, vmem_limit_bytes=VMEM_LIMIT)


def _rms(x, g):
    ms = jnp.mean(x * x, axis=-1, keepdims=True)
    return x * lax.rsqrt(ms + RMS_EPS) * g


def _dot(a, b):
    return jnp.dot(a, b, preferred_element_type=F32)


def _pick(n, pref):
    t = min(n, pref)
    assert n % t == 0, (n, pref)
    return t


def _ffn_kernel(x_ref, g_ref, w1_ref, w3_ref, w2_ref, *rest, n_f, final_norm):
    if final_norm:
        fg_ref, o_ref, xn_ref = rest
    else:
        o_ref, xn_ref = rest
    f = pl.program_id(1)

    @pl.when(f == 0)
    def _():
        xn_ref[...] = _rms(x_ref[...], g_ref[...]).astype(BF16)

    xn = xn_ref[...]
    h1 = _dot(xn, w1_ref[...])
    h3 = _dot(xn, w3_ref[...])
    a = (h1 * jax.nn.sigmoid(h1) * h3).astype(BF16)
    part = _dot(a, w2_ref[...])

    @pl.when(f == 0)
    def _():
        o_ref[...] = part

    @pl.when(f > 0)
    def _():
        o_ref[...] += part

    @pl.when(f == n_f - 1)
    def _():
        y = x_ref[...] + 0.5 * o_ref[...]
        if final_norm:
            y = _rms(y, fg_ref[...])
        o_ref[...] = y


def _ffn(x, g, w1, w3, w2, final_g=None, *, tm=512, tf=512):
    s, d = x.shape
    ff = w1.shape[1]
    tm, tf = _pick(s, tm), _pick(ff, tf)
    n_f = ff // tf
    final_norm = final_g is not None
    in_specs = [
        pl.BlockSpec((tm, d), lambda i, f: (i, 0)),
        pl.BlockSpec((1, d), lambda i, f: (0, 0)),
        pl.BlockSpec((d, tf), lambda i, f: (0, f)),
        pl.BlockSpec((d, tf), lambda i, f: (0, f)),
        pl.BlockSpec((tf, d), lambda i, f: (f, 0)),
    ]
    args = [x, g.reshape(1, d), w1, w3, w2]
    if final_norm:
        in_specs.append(pl.BlockSpec((1, d), lambda i, f: (0, 0)))
        args.append(final_g.reshape(1, d))
    return pl.pallas_call(
        functools.partial(_ffn_kernel, n_f=n_f, final_norm=final_norm),
        grid=(s // tm, n_f),
        in_specs=in_specs,
        out_specs=pl.BlockSpec((tm, d), lambda i, f: (i, 0)),
        out_shape=jax.ShapeDtypeStruct((s, d), F32),
        scratch_shapes=[pltpu.VMEM((tm, d), BF16)],
        compiler_params=_params(("parallel", "arbitrary")),
        name="ffn",
    )(*args)


def _norm_matmul_kernel(x_ref, g_ref, w_ref, *rest, rope_tiles, heads_per_tile):
    if rope_tiles:
        pos_ref, inv_ref, o_ref, xn_ref, c_ref, a_ref, b_ref = rest
    else:
        o_ref, xn_ref = rest
    j = pl.program_id(1)

    @pl.when(j == 0)
    def _():
        xn_ref[...] = _rms(x_ref[...], g_ref[...]).astype(BF16)
        if rope_tiles:
            ang = pos_ref[...].astype(F32) * inv_ref[...]
            lane = lax.broadcasted_iota(jnp.int32, ang.shape, 1)
            sn = jnp.sin(ang)
            c_ref[...] = jnp.cos(ang)
            a_ref[...] = jnp.where(lane < ROPE_DIM // 2, -sn, 0.0)
            b_ref[...] = jnp.where(lane >= ROPE_DIM // 2, sn, 0.0)

    y = _dot(xn_ref[...], w_ref[...])

    if not rope_tiles:
        o_ref[...] = y.astype(o_ref.dtype)
        return

    @pl.when(j < rope_tiles)
    def _():
        c, a, b = c_ref[...], a_ref[...], b_ref[...]
        for hh in range(heads_per_tile):
            sl = slice(hh * HEAD_DIM, (hh + 1) * HEAD_DIM)
            yh = y[:, sl]
            up = pltpu.roll(yh, HEAD_DIM - ROPE_DIM // 2, 1)
            dn = pltpu.roll(yh, ROPE_DIM // 2, 1)
            o_ref[:, sl] = (yh * c + up * a + dn * b).astype(o_ref.dtype)

    @pl.when(j >= rope_tiles)
    def _():
        o_ref[...] = y.astype(o_ref.dtype)


def _rope_inv_freq():
    inv = np.zeros((1, HEAD_DIM), np.float32)
    freqs = ROPE_THETA ** (-np.arange(0, ROPE_DIM, 2, dtype=np.float64) / ROPE_DIM)
    inv[0, :ROPE_DIM // 2] = freqs
    inv[0, ROPE_DIM // 2:ROPE_DIM] = freqs
    return jnp.asarray(inv)


def _norm_matmul(x, g, w, positions=None, rope_cols=0, *, tm=1024, tn=512):
    s, d = x.shape
    n = w.shape[1]
    tm, tn = _pick(s, tm), _pick(n, tn)
    rope_tiles = rope_cols // tn
    assert rope_cols % tn == 0 and tn % HEAD_DIM == 0
    in_specs = [
        pl.BlockSpec((tm, d), lambda i, j: (i, 0)),
        pl.BlockSpec((1, d), lambda i, j: (0, 0)),
        pl.BlockSpec((d, tn), lambda i, j: (0, j)),
    ]
    args = [x, g.reshape(1, d), w]
    scratch = [pltpu.VMEM((tm, d), BF16)]
    if rope_tiles:
        in_specs += [pl.BlockSpec((tm, 1), lambda i, j: (i, 0)),
                     pl.BlockSpec((1, HEAD_DIM), lambda i, j: (0, 0))]
        args += [positions.reshape(s, 1), _rope_inv_freq()]
        scratch += [pltpu.VMEM((tm, HEAD_DIM), F32)] * 3
    return pl.pallas_call(
        functools.partial(_norm_matmul_kernel, rope_tiles=rope_tiles, heads_per_tile=tn // HEAD_DIM),
        grid=(s // tm, n // tn),
        in_specs=in_specs,
        out_specs=pl.BlockSpec((tm, tn), lambda i, j: (i, j)),
        out_shape=jax.ShapeDtypeStruct((s, n), BF16),
        scratch_shapes=scratch,
        compiler_params=_params(("parallel", "arbitrary")),
        name="norm_matmul_rope" if rope_tiles else "norm_matmul",
    )(*args)


def _mixer_kernel(zc_ref, zh_ref, h_ref, cw_ref, pw_ref, ps_ref, wo_ref, o_ref, y_ref, *, tm, cdim, pgroup):
    i = pl.program_id(0)
    has_prev = i > 0
    chunk = pgroup

    def ext(col0):
        sl = slice(col0, col0 + chunk)
        halo = jnp.where(has_prev, zh_ref[:, sl].astype(F32), 0.0)
        return jnp.concatenate([halo, zc_ref[:, sl].astype(F32)], axis=0)

    for c0 in range(0, cdim, chunk):
        cx = ext(cdim + c0) * ext(2 * cdim + c0)
        w = cw_ref[:, c0:c0 + chunk]
        conv = (w[2:3] * cx + w[1:2] * pltpu.roll(cx, 1, 0) + w[0:1] * pltpu.roll(cx, 2, 0))[HALO:]
        gate_b = zc_ref[:, c0:c0 + chunk].astype(F32)
        y_ref[:, c0:c0 + chunk] = (gate_b * conv).astype(BF16)

    t = i * tm + lax.broadcasted_iota(jnp.int32, (tm, 1), 0)
    for gi, win in enumerate(POOL_WINDOWS):
        col0 = 3 * cdim + gi * chunk
        acc = ext(col0)
        sh = 1
        while sh < win:
            acc = acc + pltpu.roll(acc, sh, 0)
            sh *= 2
        inv_cnt = 1.0 / jnp.minimum(t + 1, win).astype(F32)
        u = zc_ref[:, col0:col0 + chunk].astype(F32)
        pooled = acc[HALO:] * inv_cnt - u
        mixed = _dot(pooled.astype(BF16), pw_ref[gi])
        y_ref[:, cdim + gi * chunk: cdim + (gi + 1) * chunk] = (
            mixed * ps_ref[:, gi * chunk:(gi + 1) * chunk]).astype(BF16)

    o_ref[...] = h_ref[...] + _dot(y_ref[...], wo_ref[...])


def _mixer(z, h, conv_w, pool_w, pool_scale, w_out, *, tm=256):
    s, d = h.shape
    cdim = d // 2
    pgroup = pool_w.shape[-1]
    tm = _pick(s, tm)
    assert tm % HALO == 0 and cdim % pgroup == 0 and max(POOL_WINDOWS) <= HALO
    hb = tm // HALO
    return pl.pallas_call(
        functools.partial(_mixer_kernel, tm=tm, cdim=cdim, pgroup=pgroup),
        grid=(s // tm,),
        in_specs=[
            pl.BlockSpec((tm, 4 * cdim), lambda i: (i, 0)),
            pl.BlockSpec((HALO, 4 * cdim), lambda i: (jnp.maximum(i * hb - 1, 0), 0)),
            pl.BlockSpec((tm, d), lambda i: (i, 0)),
            pl.BlockSpec((CONV_WIDTH, cdim), lambda i: (0, 0)),
            pl.BlockSpec(pool_w.shape, lambda i: (0, 0, 0)),
            pl.BlockSpec((1, cdim), lambda i: (0, 0)),
            pl.BlockSpec((d, d), lambda i: (0, 0)),
        ],
        out_specs=pl.BlockSpec((tm, d), lambda i: (i, 0)),
        out_shape=jax.ShapeDtypeStruct((s, d), F32),
        scratch_shapes=[pltpu.VMEM((tm, d), BF16)],
        compiler_params=_params(("parallel",)),
        name="mixer",
    )(z, z, h, conv_w, pool_w, pool_scale.reshape(1, cdim), w_out)


def _attn_kernel(q_ref, kc_ref, vc_ref, kp_ref, vp_ref, o_ref, lse_ref, *, tq, n_heads, span):
    n = pl.program_id(1)
    blk = ATT_BLOCK
    a_i = lax.broadcasted_iota(jnp.int32, (blk, 2 * blk), 0)
    c_i = lax.broadcasted_iota(jnp.int32, (blk, 2 * blk), 1)
    dist = a_i + blk - c_i
    band = (dist >= 0) & (dist <= span)
    scale = 1.0 / np.sqrt(HEAD_DIM)
    lse_ref[...] = jnp.zeros(lse_ref.shape, F32)

    for b in range(tq // blk):
        rows = slice(b * blk, (b + 1) * blk)
        if b == 0:
            valid = band & (c_i >= jnp.where(n > 0, 0, blk))
        else:
            valid = band
        for h in range(n_heads):
            cols = slice(h * HEAD_DIM, (h + 1) * HEAD_DIM)
            q = q_ref[rows, cols]
            if b == 0:
                kk = jnp.concatenate([kp_ref[:, cols], kc_ref[rows, cols]], axis=0)
                vv = jnp.concatenate([vp_ref[:, cols], vc_ref[rows, cols]], axis=0)
            else:
                kk = kc_ref[(b - 1) * blk:(b + 1) * blk, cols]
                vv = vc_ref[(b - 1) * blk:(b + 1) * blk, cols]
            s = lax.dot_general(q, kk, (((1,), (1,)), ((), ())), preferred_element_type=F32) * scale
            s = jnp.where(valid, s, NEG_INF)
            m = jnp.max(s, axis=-1, keepdims=True)
            p = jnp.exp(s - m)
            l = jnp.sum(p, axis=-1, keepdims=True)
            o = _dot(p.astype(BF16), vv) / l
            o_ref[rows, cols] = o.astype(o_ref.dtype)
            lse_ref[rows, h:h + 1] = m + jnp.log(l)


def _attn_branch(qkv, dil, span, *, tq=512):
    s, c3 = qkv.shape
    d = c3 // 3
    n_heads = d // HEAD_DIM
    ln = s // dil
    tq = _pick(ln, tq)
    assert ln % ATT_BLOCK == 0 and tq % ATT_BLOCK == 0
    qkv2 = qkv.reshape(ln, dil * c3)
    pb = tq // ATT_BLOCK
    cur = lambda col: pl.BlockSpec((tq, d), lambda r, n: (n, 3 * r + col))
    prev = lambda col: pl.BlockSpec((ATT_BLOCK, d), lambda r, n: (jnp.maximum(n * pb - 1, 0), 3 * r + col))
    o, lse = pl.pallas_call(
        functools.partial(_attn_kernel, tq=tq, n_heads=n_heads, span=span),
        grid=(dil, ln // tq),
        in_specs=[cur(0), cur(1), cur(2), prev(1), prev(2)],
        out_specs=[pl.BlockSpec((tq, d), lambda r, n: (n, r)),
                   pl.BlockSpec((tq, LSE_LANES), lambda r, n: (n, r))],
        out_shape=[jax.ShapeDtypeStruct((ln, dil * d), BF16),
                   jax.ShapeDtypeStruct((ln, dil * LSE_LANES), F32)],
        compiler_params=_params(("parallel", "parallel")),
        name=f"attn_dil{dil}",
    )(qkv2, qkv2, qkv2, qkv2, qkv2)
    return o.reshape(s, d), lse.reshape(s, LSE_LANES)


def _attn_out_kernel(*refs, n_br, n_heads):
    o_refs, lse_refs = refs[:n_br], refs[n_br:2 * n_br]
    h_ref, wo_ref, out_ref, comb_ref = refs[2 * n_br:]
    lses = [r[...] for r in lse_refs]
    m = functools.reduce(jnp.maximum, lses)
    es = [jnp.exp(x - m) for x in lses]
    tot = functools.reduce(lambda p, q: p + q, es)
    wts = [e / tot for e in es]
    for h in range(n_heads):
        cols = slice(h * HEAD_DIM, (h + 1) * HEAD_DIM)
        acc = None
        for g in range(n_br):
            term = wts[g][:, h:h + 1] * o_refs[g][:, cols].astype(F32)
            acc = term if acc is None else acc + term
        comb_ref[:, cols] = acc.astype(BF16)
    out_ref[...] = h_ref[...] + _dot(comb_ref[...], wo_ref[...])


def _attn_out(os_, lses, h, w_o, *, tm=512):
    s, d = h.shape
    n_heads = d // HEAD_DIM
    tm = _pick(s, tm)
    n_br = len(os_)
    row = lambda w: pl.BlockSpec((tm, w), lambda i: (i, 0))
    return pl.pallas_call(
        functools.partial(_attn_out_kernel, n_br=n_br, n_heads=n_heads),
        grid=(s // tm,),
        in_specs=[row(d)] * n_br + [row(LSE_LANES)] * n_br + [row(d), pl.BlockSpec((d, d), lambda i: (0, 0))],
        out_specs=row(d),
        out_shape=jax.ShapeDtypeStruct((s, d), F32),
        scratch_shapes=[pltpu.VMEM((tm, d), BF16)],
        compiler_params=_params(("parallel",)),
        name="attn_out",
    )(*os_, *lses, h, w_o)


def kernel(x, positions, norm_g, ffn_w1, ffn_w3, ffn_w2, mix_w_in, conv_w, pool_w, pool_scale,
           mix_w_out, attn_w_qkv, attn_w_o, final_g):
    b, s, d = x.shape
    depth = norm_g.shape[0]
    bf = lambda w: w.astype(BF16)
    outs = []
    for bi in range(b):
        h = x[bi]
        pos = positions[bi]
        for i in range(depth):
            h = _ffn(h, norm_g[i, 0], bf(ffn_w1[i, 0]), bf(ffn_w3[i, 0]), bf(ffn_w2[i, 0]))
            j = i // 2
            if i % 2 == 0:
                z = _norm_matmul(h, norm_g[i, 1], bf(mix_w_in[j]))
                h = _mixer(z, h, conv_w[j], bf(pool_w[j]), pool_scale[j], bf(mix_w_out[j]))
            else:
                qkv = _norm_matmul(h, norm_g[i, 1], bf(attn_w_qkv[j]), positions=pos, rope_cols=2 * d)
                os_, lses = zip(*[_attn_branch(qkv, dil, win // dil) for win, dil in DIL_BRANCHES])
                h = _attn_out(os_, lses, h, bf(attn_w_o[j]))
            last = i == depth - 1
            h = _ffn(h, norm_g[i, 2], bf(ffn_w1[i, 1]), bf(ffn_w3[i, 1]), bf(ffn_w2[i, 1]),
                     final_g=final_g if last else None)
        outs.append(h.reshape(1, s, d))
    return outs[0] if b == 1 else jnp.concatenate(outs, axis=0)
```

```python
import functools

import numpy as np
import jax
import jax.numpy as jnp
from jax import lax
from jax.experimental import pallas as pl
from jax.experimental.pallas import tpu as pltpu

F32 = jnp.float32
BF16 = jnp.bfloat16

HEAD_DIM = 128
ROPE_DIM = HEAD_DIM // 4
ROPE_THETA = 500000.0
DIL_BRANCHES = ((128, 1), (512, 4), (2048, 16))
ATT_BLOCK = 128
ATT_TILE = ATT_BLOCK * max(d for _, d in DIL_BRANCHES)
COMBINE_ROWS = 256
EPILOGUE_ROWS = 256
CONV_WIDTH = 3
POOL_WINDOWS = (2, 4, 8, 16)
HALO = 16
RMS_EPS = 1e-6
NEG_INF = -1e30
LOG2E = float(np.log2(np.e))
LN2 = float(np.log(2.0))

MXU_COLS = 256
V7X_VMEM_BYTES = 64 * 1024 * 1024
VMEM_LIMIT = V7X_VMEM_BYTES - 6 * 1024 * 1024


def _params(sem):
    return pltpu.CompilerParams(dimension_semantics=sem, vmem_limit_bytes=VMEM_LIMIT)


def _rms(x, g):
    ms = jnp.mean(x * x, axis=-1, keepdims=True)
    return x * lax.rsqrt(ms + RMS_EPS) * g


def _dot(a, b):
    return jnp.dot(a, b, preferred_element_type=F32)


def _pick(n, pref):
    t = min(n, pref)
    assert n % t == 0, (n, pref)
    return t


def _ffn_kernel(x_ref, g_ref, w1_ref, w3_ref, w2_ref, *rest, n_f, final_norm):
    if final_norm:
        fg_ref, o_ref, xn_ref = rest
    else:
        o_ref, xn_ref = rest
    f = pl.program_id(1)

    @pl.when(f == 0)
    def _():
        xn_ref[...] = _rms(x_ref[...], g_ref[...]).astype(BF16)
        o_ref[...] = jnp.zeros(o_ref.shape, F32)

    xn = xn_ref[...]
    h1 = _dot(xn, w1_ref[...])
    h3 = _dot(xn, w3_ref[...])
    a = (h1 * jax.nn.sigmoid(h1) * h3).astype(BF16)
    o_ref[...] += _dot(a, w2_ref[...])

    @pl.when(f == n_f - 1)
    def _():
        def chunk(c, carry):
            rows = pl.ds(pl.multiple_of(c * EPILOGUE_ROWS, EPILOGUE_ROWS), EPILOGUE_ROWS)
            y = x_ref[rows, :] + 0.5 * o_ref[rows, :]
            if final_norm:
                y = _rms(y, fg_ref[...])
            o_ref[rows, :] = y
            return carry

        lax.fori_loop(0, o_ref.shape[0] // EPILOGUE_ROWS, chunk, 0)


def _ffn(x, g, w1, w3, w2, layer, final_g=None, *, tm=1024, tf=512):
    s, d = x.shape
    ff = w1.shape[-1]
    tm, tf = _pick(s, tm), _pick(ff, tf)
    n_f = ff // tf
    li, lj = layer
    final_norm = final_g is not None
    in_specs = [
        pl.BlockSpec((tm, d), lambda i, f: (i, 0)),
        pl.BlockSpec((1, d), lambda i, f: (0, 0)),
        pl.BlockSpec((None, None, d, tf), lambda i, f: (li, lj, 0, f)),
        pl.BlockSpec((None, None, d, tf), lambda i, f: (li, lj, 0, f)),
        pl.BlockSpec((None, None, tf, d), lambda i, f: (li, lj, f, 0)),
    ]
    args = [x, g.reshape(1, d), w1, w3, w2]
    if final_norm:
        in_specs.append(pl.BlockSpec((1, d), lambda i, f: (0, 0)))
        args.append(final_g.reshape(1, d))
    return pl.pallas_call(
        functools.partial(_ffn_kernel, n_f=n_f, final_norm=final_norm),
        grid=(s // tm, n_f),
        in_specs=in_specs,
        out_specs=pl.BlockSpec((tm, d), lambda i, f: (i, 0)),
        out_shape=jax.ShapeDtypeStruct((s, d), F32),
        scratch_shapes=[pltpu.VMEM((tm, d), BF16)],
        compiler_params=_params(("parallel", "arbitrary")),
        name="ffn",
    )(*args)


def _norm_matmul_kernel(x_ref, g_ref, w_ref, *rest, rope, tiles_per_part, heads_per_tile, q_scale):
    if rope:
        pos_ref, inv_ref, o_ref, xn_ref, tab_ref = rest
    else:
        o_ref, xn_ref = rest
    j = pl.program_id(1)

    @pl.when(j == 0)
    def _():
        xn_ref[...] = _rms(x_ref[...], g_ref[...]).astype(BF16)
        if rope:
            ang = pos_ref[...].astype(F32) * inv_ref[...]
            lane = lax.broadcasted_iota(jnp.int32, ang.shape, 1)
            sn = jnp.sin(ang)
            c = jnp.cos(ang)
            a = jnp.where(lane < ROPE_DIM // 2, -sn, 0.0)
            b = jnp.where(lane >= ROPE_DIM // 2, sn, 0.0)
            for idx, tab in enumerate((c, a, b)):
                tab_ref[0, idx] = tab * q_scale
                tab_ref[1, idx] = tab
            tab_ref[2, 0] = jnp.ones(ang.shape, F32)
            tab_ref[2, 1] = jnp.zeros(ang.shape, F32)
            tab_ref[2, 2] = jnp.zeros(ang.shape, F32)

    if not rope:
        o_ref[...] = _dot(xn_ref[...], w_ref[...]).astype(o_ref.dtype)
        return

    part = j // tiles_per_part
    for c0 in range(0, heads_per_tile * HEAD_DIM, MXU_COLS):
        y = _dot(xn_ref[...], w_ref[:, c0:c0 + MXU_COLS])
        for h0 in range(0, MXU_COLS, HEAD_DIM):
            yh = y[:, h0:h0 + HEAD_DIM]
            up = pltpu.roll(yh, HEAD_DIM - ROPE_DIM // 2, 1)
            dn = pltpu.roll(yh, ROPE_DIM // 2, 1)
            out = yh * tab_ref[part, 0] + up * tab_ref[part, 1] + dn * tab_ref[part, 2]
            o_ref[:, c0 + h0:c0 + h0 + HEAD_DIM] = out.astype(o_ref.dtype)


def _rope_inv_freq():
    inv = np.zeros((1, HEAD_DIM), np.float32)
    freqs = ROPE_THETA ** (-np.arange(0, ROPE_DIM, 2, dtype=np.float64) / ROPE_DIM)
    inv[0, :ROPE_DIM // 2] = freqs
    inv[0, ROPE_DIM // 2:ROPE_DIM] = freqs
    return jnp.asarray(inv)


def _norm_matmul(x, g, w, positions=None, *, out_dtype=BF16, tm=1024, tn=512):
    s, d = x.shape
    n = w.shape[1]
    tm, tn = _pick(s, tm), _pick(n, tn)
    rope = positions is not None
    in_specs = [
        pl.BlockSpec((tm, d), lambda i, j: (i, 0)),
        pl.BlockSpec((1, d), lambda i, j: (0, 0)),
        pl.BlockSpec((d, tn), lambda i, j: (0, j)),
    ]
    args = [x, g.reshape(1, d), w]
    scratch = [pltpu.VMEM((tm, d), BF16)]
    tiles_per_part = 0
    if rope:
        assert n % 3 == 0 and (n // 3) % tn == 0 and tn % MXU_COLS == 0 and MXU_COLS % HEAD_DIM == 0
        tiles_per_part = n // 3 // tn
        in_specs += [pl.BlockSpec((tm, 1), lambda i, j: (i, 0)),
                     pl.BlockSpec((1, HEAD_DIM), lambda i, j: (0, 0))]
        args += [positions.reshape(s, 1), _rope_inv_freq()]
        scratch.append(pltpu.VMEM((3, 3, tm, HEAD_DIM), F32))
    return pl.pallas_call(
        functools.partial(_norm_matmul_kernel, rope=rope, tiles_per_part=tiles_per_part,
                          heads_per_tile=tn // HEAD_DIM, q_scale=LOG2E / np.sqrt(HEAD_DIM)),
        grid=(s // tm, n // tn),
        in_specs=in_specs,
        out_specs=pl.BlockSpec((tm, tn), lambda i, j: (i, j)),
        out_shape=jax.ShapeDtypeStruct((s, n), out_dtype),
        scratch_shapes=scratch,
        compiler_params=_params(("parallel", "arbitrary")),
        name="norm_matmul_rope" if rope else "norm_matmul",
    )(*args)


def _proj_residual_kernel(a_ref, h_ref, w_ref, o_ref):
    o_ref[...] = h_ref[...] + _dot(a_ref[...], w_ref[...])


def _proj_residual(a, h, w, *, tm=512):
    s, d = h.shape
    tm = _pick(s, tm)
    row = lambda width: pl.BlockSpec((tm, width), lambda i: (i, 0))
    return pl.pallas_call(
        _proj_residual_kernel,
        grid=(s // tm,),
        in_specs=[row(a.shape[1]), row(d), pl.BlockSpec(w.shape, lambda i: (0, 0))],
        out_specs=row(d),
        out_shape=jax.ShapeDtypeStruct((s, d), F32),
        compiler_params=_params(("parallel",)),
        name="proj_residual",
    )(a, h, w)


def _mixer_kernel(zc_ref, zh_ref, h_ref, cw_ref, pw_ref, ps_ref, wo_ref, o_ref, y_ref, *, tm, cdim, pgroup):
    i = pl.program_id(0)
    has_prev = i > 0
    chunk = pgroup

    def ext(col0):
        sl = slice(col0, col0 + chunk)
        halo = jnp.where(has_prev, zh_ref[:, sl].astype(F32), 0.0)
        return jnp.concatenate([halo, zc_ref[:, sl].astype(F32)], axis=0)

    for c0 in range(0, cdim, chunk):
        cx = ext(cdim + c0) * ext(2 * cdim + c0)
        w = cw_ref[:, c0:c0 + chunk]
        conv = (w[2:3] * cx + w[1:2] * pltpu.roll(cx, 1, 0) + w[0:1] * pltpu.roll(cx, 2, 0))[HALO:]
        gate_b = zc_ref[:, c0:c0 + chunk].astype(F32)
        y_ref[:, c0:c0 + chunk] = (gate_b * conv).astype(BF16)

    t = i * tm + lax.broadcasted_iota(jnp.int32, (tm, 1), 0)
    for gi, win in enumerate(POOL_WINDOWS):
        col0 = 3 * cdim + gi * chunk
        acc = ext(col0)
        sh = 1
        while sh < win:
            acc = acc + pltpu.roll(acc, sh, 0)
            sh *= 2
        inv_cnt = 1.0 / jnp.minimum(t + 1, win).astype(F32)
        u = zc_ref[:, col0:col0 + chunk].astype(F32)
        pooled = acc[HALO:] * inv_cnt - u
        mixed = _dot(pooled.astype(BF16), pw_ref[gi])
        y_ref[:, cdim + gi * chunk: cdim + (gi + 1) * chunk] = (
            mixed * ps_ref[:, gi * chunk:(gi + 1) * chunk]).astype(BF16)

    o_ref[...] = h_ref[...] + _dot(y_ref[...], wo_ref[...])


def _mixer(z, h, conv_w, pool_w, pool_scale, w_out, *, tm=256):
    s, d = h.shape
    cdim = d // 2
    pgroup = pool_w.shape[-1]
    tm = _pick(s, tm)
    assert tm % HALO == 0 and cdim % pgroup == 0 and max(POOL_WINDOWS) <= HALO
    hb = tm // HALO
    return pl.pallas_call(
        functools.partial(_mixer_kernel, tm=tm, cdim=cdim, pgroup=pgroup),
        grid=(s // tm,),
        in_specs=[
            pl.BlockSpec((tm, 4 * cdim), lambda i: (i, 0)),
            pl.BlockSpec((HALO, 4 * cdim), lambda i: (jnp.maximum(i * hb - 1, 0), 0)),
            pl.BlockSpec((tm, d), lambda i: (i, 0)),
            pl.BlockSpec((CONV_WIDTH, cdim), lambda i: (0, 0)),
            pl.BlockSpec(pool_w.shape, lambda i: (0, 0, 0)),
            pl.BlockSpec((1, cdim), lambda i: (0, 0)),
            pl.BlockSpec((d, d), lambda i: (0, 0)),
        ],
        out_specs=pl.BlockSpec((tm, d), lambda i: (i, 0)),
        out_shape=jax.ShapeDtypeStruct((s, d), F32),
        scratch_shapes=[pltpu.VMEM((tm, d), BF16)],
        compiler_params=_params(("parallel",)),
        name="mixer",
    )(z, z, h, conv_w, pool_w, pool_scale.reshape(1, cdim), w_out)


def _attn_kernel(q_ref, k_ref, v_ref, o_ref, *scr, dils, span):
    n_br = len(dils)
    ks, vs, qs, ogs, lgs = (scr[i * n_br:(i + 1) * n_br] for i in range(5))
    t = pl.program_id(1)
    blk = ATT_BLOCK
    tile = q_ref.shape[0]

    @pl.when(t == 0)
    def _():
        for kd, vd in zip(ks, vs):
            kd[:, :blk, :] = jnp.zeros((kd.shape[0], blk, HEAD_DIM), BF16)
            vd[:, :blk, :] = jnp.zeros((vd.shape[0], blk, HEAD_DIM), BF16)

    @pl.when(t > 0)
    def _():
        for kd, vd, d in zip(ks, vs, dils):
            n = tile // d
            kd[:, :blk, :] = kd[:, n:n + blk, :]
            vd[:, :blk, :] = vd[:, n:n + blk, :]

    for kd, vd, qd, d in zip(ks, vs, qs, dils):
        n = tile // d
        for r in range(d):
            rows = pl.ds(r, n, stride=d) if d > 1 else slice(None)
            kd[r, blk:, :] = k_ref[rows, :].astype(BF16)
            vd[r, blk:, :] = v_ref[rows, :].astype(BF16)
            qd[r] = q_ref[rows, :].astype(BF16)

    a_i = lax.broadcasted_iota(jnp.int32, (blk, 2 * blk), 0)
    c_i = lax.broadcasted_iota(jnp.int32, (blk, 2 * blk), 1)
    dist = a_i + blk - c_i
    band = (dist >= 0) & (dist <= span)
    band_first = band & (c_i >= jnp.where(t == 0, blk, 0))

    for g, d in enumerate(dils):
        n = tile // d
        for r in range(d):
            for b in range(n // blk):
                q = qs[g][r, b * blk:(b + 1) * blk, :]
                kk = ks[g][r, b * blk:(b + 2) * blk, :]
                vv = vs[g][r, b * blk:(b + 2) * blk, :]
                s = lax.dot_general(q, kk, (((1,), (1,)), ((), ())), preferred_element_type=F32)
                s = jnp.where(band_first if b == 0 else band, s, NEG_INF)
                m = jnp.max(s, axis=-1, keepdims=True)
                p = jnp.exp2(s - m)
                l = jnp.sum(p, axis=-1, keepdims=True)
                o = _dot(p.astype(BF16), vv) / l
                lse = m * LN2 + jnp.log(l)
                rows = pl.ds(b * blk * d + r, blk, stride=d) if d > 1 else slice(b * blk, (b + 1) * blk)
                ogs[g][rows, :] = o
                lgs[g][rows, :] = jnp.broadcast_to(lse, (blk, HEAD_DIM))

    for c0 in range(0, tile, COMBINE_ROWS):
        rows = slice(c0, c0 + COMBINE_ROWS)
        ls = [lg[rows, :] for lg in lgs]
        m = functools.reduce(jnp.maximum, ls)
        es = [jnp.exp(x - m) for x in ls]
        tot = functools.reduce(lambda x, y: x + y, es)
        acc = functools.reduce(lambda x, y: x + y, [e * og[rows, :] for e, og in zip(es, ogs)])
        o_ref[rows, :] = (acc / tot).astype(o_ref.dtype)


def _attention(qkv, *, tile=ATT_TILE):
    s, c3 = qkv.shape
    d_model = c3 // 3
    n_heads = d_model // HEAD_DIM
    dils = tuple(d for _, d in DIL_BRANCHES)
    spans = {w // d for w, d in DIL_BRANCHES}
    assert len(spans) == 1 and s % tile == 0 and all(tile % (ATT_BLOCK * d) == 0 for d in dils)
    span = spans.pop()
    assert span <= ATT_BLOCK
    kv = [pltpu.VMEM((d, ATT_BLOCK + tile // d, HEAD_DIM), BF16) for d in dils]
    qd = [pltpu.VMEM((d, tile // d, HEAD_DIM), BF16) for d in dils]
    acc = [pltpu.VMEM((tile, HEAD_DIM), F32) for _ in dils]
    blockspec = lambda part: pl.BlockSpec((tile, HEAD_DIM), lambda h, t: (t, part * n_heads + h))
    return pl.pallas_call(
        functools.partial(_attn_kernel, dils=dils, span=span),
        grid=(n_heads, s // tile),
        in_specs=[blockspec(0), blockspec(1), blockspec(2)],
        out_specs=pl.BlockSpec((tile, HEAD_DIM), lambda h, t: (t, h)),
        out_shape=jax.ShapeDtypeStruct((s, d_model), BF16),
        scratch_shapes=kv + kv + qd + acc + acc,
        compiler_params=_params(("parallel", "arbitrary")),
        name="attn",
    )(qkv, qkv, qkv)


def kernel(x, positions, norm_g, ffn_w1, ffn_w3, ffn_w2, mix_w_in, conv_w, pool_w, pool_scale,
           mix_w_out, attn_w_qkv, attn_w_o, final_g):
    b, s, d = x.shape
    depth = norm_g.shape[0]
    bf = lambda w: w.astype(BF16)
    w1, w3, w2 = bf(ffn_w1), bf(ffn_w3), bf(ffn_w2)
    outs = []
    for bi in range(b):
        h = x[bi]
        for i in range(depth):
            h = _ffn(h, norm_g[i, 0], w1, w3, w2, (i, 0))
            j = i // 2
            if i % 2 == 0:
                z = _norm_matmul(h, norm_g[i, 1], bf(mix_w_in[j]))
                h = _mixer(z, h, conv_w[j], bf(pool_w[j]), pool_scale[j], bf(mix_w_out[j]))
            else:
                qkv = _norm_matmul(h, norm_g[i, 1], bf(attn_w_qkv[j]), positions[bi], out_dtype=F32, tn=1024)
                h = _proj_residual(_attention(qkv), h, bf(attn_w_o[j]))
            h = _ffn(h, norm_g[i, 2], w1, w3, w2, (i, 1), final_g=final_g if i == depth - 1 else None)
        outs.append(h.reshape(1, s, d))
    return outs[0] if b == 1 else jnp.concatenate(outs, axis=0)
```

```python
import functools

import numpy as np
import jax
import jax.numpy as jnp
from jax import lax
from jax.experimental import pallas as pl
from jax.experimental.pallas import tpu as pltpu

F32 = jnp.float32
BF16 = jnp.bfloat16

HEAD_DIM = 128
ROPE_DIM = HEAD_DIM // 4
ROPE_THETA = 500000.0
DIL_BRANCHES = ((128, 1), (512, 4), (2048, 16))
ATT_BLOCK = 128
ATT_TILE = ATT_BLOCK * max(d for _, d in DIL_BRANCHES)
COMBINE_ROWS = 256
EPILOGUE_ROWS = 256
CONV_WIDTH = 3
POOL_WINDOWS = (2, 4, 8, 16)
HALO = 16
RMS_EPS = 1e-6
NEG_INF = -1e30
LOG2E = float(np.log2(np.e))
LN2 = float(np.log(2.0))

MXU_COLS = 256
V7X_VMEM_BYTES = 64 * 1024 * 1024
VMEM_LIMIT = V7X_VMEM_BYTES - 6 * 1024 * 1024


def _params(sem):
    return pltpu.CompilerParams(dimension_semantics=sem, vmem_limit_bytes=VMEM_LIMIT)


def _rms(x, g):
    ms = jnp.mean(x * x, axis=-1, keepdims=True)
    return x * lax.rsqrt(ms + RMS_EPS) * g


def _dot(a, b):
    return jnp.dot(a, b, preferred_element_type=F32)


def _pick(n, pref):
    t = min(n, pref)
    while n % t:
        t //= 2
    return t


def _ffn_kernel(x_ref, g_ref, w1_ref, w3_ref, w2_ref, *rest, n_f, final_norm, n_cast):
    rest = list(rest)
    fg_ref = rest.pop(0) if final_norm else None
    cast_in = [rest.pop(0) for _ in range(n_cast)]
    o_ref = rest.pop(0)
    cast_out = [rest.pop(0) for _ in range(n_cast)]
    (xn_ref,) = rest
    f = pl.program_id(1)

    @pl.when(f == 0)
    def _():
        xn_ref[...] = _rms(x_ref[...], g_ref[...]).astype(BF16)
        o_ref[...] = jnp.zeros(o_ref.shape, F32)

    xn = xn_ref[...]
    h1 = _dot(xn, w1_ref[...])
    h3 = _dot(xn, w3_ref[...])
    a = (h1 * jax.nn.sigmoid(h1) * h3).astype(BF16)
    o_ref[...] += _dot(a, w2_ref[...])

    for src, dst in zip(cast_in, cast_out):
        dst[...] = src[...].astype(BF16)

    @pl.when(f == n_f - 1)
    def _():
        def chunk(c, carry):
            rows = pl.ds(pl.multiple_of(c * EPILOGUE_ROWS, EPILOGUE_ROWS), EPILOGUE_ROWS)
            y = x_ref[rows, :] + 0.5 * o_ref[rows, :]
            if final_norm:
                y = _rms(y, fg_ref[...])
            o_ref[rows, :] = y
            return carry

        lax.fori_loop(0, o_ref.shape[0] // EPILOGUE_ROWS, chunk, 0)


def _cast_specs(w, lead, n_i, n_f):
    r, c = w.shape[-2:]
    wide = c >= r
    shape = (r // n_i, c // n_f) if wide else (r // n_f, c // n_i)
    assert shape[0] * (n_i if wide else n_f) == r and shape[1] * (n_f if wide else n_i) == c
    assert shape[0] % 16 == 0 and shape[1] % 128 == 0, shape
    block = (lambda i, f: (i, f)) if wide else (lambda i, f: (f, i))
    src = pl.BlockSpec((None,) * len(lead) + shape, lambda i, f: tuple(lead) + block(i, f))
    return src, pl.BlockSpec(shape, block)


def _ffn(x, g, w1, w3, w2, final_g=None, cast=(), *, tm=1024, tf=512):
    s, d = x.shape
    ff = w1.shape[-1]
    tm, tf = _pick(s, tm), _pick(ff, tf)
    n_i, n_f = s // tm, ff // tf
    final_norm = final_g is not None
    in_specs = [
        pl.BlockSpec((tm, d), lambda i, f: (i, 0)),
        pl.BlockSpec((1, d), lambda i, f: (0, 0)),
        pl.BlockSpec((d, tf), lambda i, f: (0, f)),
        pl.BlockSpec((d, tf), lambda i, f: (0, f)),
        pl.BlockSpec((tf, d), lambda i, f: (f, 0)),
    ]
    args = [x, g.reshape(1, d), w1, w3, w2]
    if final_norm:
        in_specs.append(pl.BlockSpec((1, d), lambda i, f: (0, 0)))
        args.append(final_g.reshape(1, d))
    cast_specs = [_cast_specs(w, lead, n_i, n_f) for w, lead in cast]
    outs = pl.pallas_call(
        functools.partial(_ffn_kernel, n_f=n_f, final_norm=final_norm, n_cast=len(cast)),
        grid=(n_i, n_f),
        in_specs=in_specs + [src for src, _ in cast_specs],
        out_specs=[pl.BlockSpec((tm, d), lambda i, f: (i, 0))] + [dst for _, dst in cast_specs],
        out_shape=[jax.ShapeDtypeStruct((s, d), F32)]
        + [jax.ShapeDtypeStruct(w.shape[-2:], BF16) for w, _ in cast],
        scratch_shapes=[pltpu.VMEM((tm, d), BF16)],
        compiler_params=_params(("parallel", "arbitrary")),
        name="ffn",
    )(*args, *[w for w, _ in cast])
    return outs[0] if not cast else outs


def _norm_matmul_kernel(x_ref, g_ref, w_ref, *rest, rope, tiles_per_part, heads_per_tile, q_scale):
    if rope:
        pos_ref, inv_ref, o_ref, xn_ref, tab_ref = rest
    else:
        o_ref, xn_ref = rest
    j = pl.program_id(1)

    @pl.when(j == 0)
    def _():
        xn_ref[...] = _rms(x_ref[...], g_ref[...]).astype(BF16)
        if rope:
            ang = pos_ref[...].astype(F32) * inv_ref[...]
            lane = lax.broadcasted_iota(jnp.int32, ang.shape, 1)
            sn = jnp.sin(ang)
            c = jnp.cos(ang)
            a = jnp.where(lane < ROPE_DIM // 2, -sn, 0.0)
            b = jnp.where(lane >= ROPE_DIM // 2, sn, 0.0)
            for idx, tab in enumerate((c, a, b)):
                tab_ref[0, idx] = tab * q_scale
                tab_ref[1, idx] = tab
            tab_ref[2, 0] = jnp.ones(ang.shape, F32)
            tab_ref[2, 1] = jnp.zeros(ang.shape, F32)
            tab_ref[2, 2] = jnp.zeros(ang.shape, F32)

    if not rope:
        o_ref[...] = _dot(xn_ref[...], w_ref[...]).astype(o_ref.dtype)
        return

    part = j // tiles_per_part
    for c0 in range(0, heads_per_tile * HEAD_DIM, MXU_COLS):
        y = _dot(xn_ref[...], w_ref[:, c0:c0 + MXU_COLS])
        for h0 in range(0, MXU_COLS, HEAD_DIM):
            yh = y[:, h0:h0 + HEAD_DIM]
            up = pltpu.roll(yh, HEAD_DIM - ROPE_DIM // 2, 1)
            dn = pltpu.roll(yh, ROPE_DIM // 2, 1)
            out = yh * tab_ref[part, 0] + up * tab_ref[part, 1] + dn * tab_ref[part, 2]
            o_ref[:, c0 + h0:c0 + h0 + HEAD_DIM] = out.astype(o_ref.dtype)


def _rope_inv_freq():
    inv = np.zeros((1, HEAD_DIM), np.float32)
    freqs = ROPE_THETA ** (-np.arange(0, ROPE_DIM, 2, dtype=np.float64) / ROPE_DIM)
    inv[0, :ROPE_DIM // 2] = freqs
    inv[0, ROPE_DIM // 2:ROPE_DIM] = freqs
    return jnp.asarray(inv)


def _norm_matmul(x, g, w, positions=None, *, out_dtype=BF16, tm=1024, tn=512):
    s, d = x.shape
    n = w.shape[1]
    tm, tn = _pick(s, tm), _pick(n, tn)
    rope = positions is not None
    in_specs = [
        pl.BlockSpec((tm, d), lambda i, j: (i, 0)),
        pl.BlockSpec((1, d), lambda i, j: (0, 0)),
        pl.BlockSpec((d, tn), lambda i, j: (0, j)),
    ]
    args = [x, g.reshape(1, d), w]
    scratch = [pltpu.VMEM((tm, d), BF16)]
    tiles_per_part = 0
    if rope:
        assert n % 3 == 0 and (n // 3) % tn == 0 and tn % MXU_COLS == 0 and MXU_COLS % HEAD_DIM == 0
        tiles_per_part = n // 3 // tn
        in_specs += [pl.BlockSpec((tm, 1), lambda i, j: (i, 0)),
                     pl.BlockSpec((1, HEAD_DIM), lambda i, j: (0, 0))]
        args += [positions.reshape(s, 1), _rope_inv_freq()]
        scratch.append(pltpu.VMEM((3, 3, tm, HEAD_DIM), F32))
    return pl.pallas_call(
        functools.partial(_norm_matmul_kernel, rope=rope, tiles_per_part=tiles_per_part,
                          heads_per_tile=tn // HEAD_DIM, q_scale=LOG2E / np.sqrt(HEAD_DIM)),
        grid=(s // tm, n // tn),
        in_specs=in_specs,
        out_specs=pl.BlockSpec((tm, tn), lambda i, j: (i, j)),
        out_shape=jax.ShapeDtypeStruct((s, n), out_dtype),
        scratch_shapes=scratch,
        compiler_params=_params(("parallel", "arbitrary")),
        name="norm_matmul_rope" if rope else "norm_matmul",
    )(*args)


def _proj_residual_kernel(a_ref, h_ref, w_ref, o_ref):
    o_ref[...] = h_ref[...] + _dot(a_ref[...], w_ref[...])


def _proj_residual(a, h, w, *, tm=512):
    s, d = h.shape
    tm = _pick(s, tm)
    row = lambda width: pl.BlockSpec((tm, width), lambda i: (i, 0))
    return pl.pallas_call(
        _proj_residual_kernel,
        grid=(s // tm,),
        in_specs=[row(a.shape[1]), row(d), pl.BlockSpec(w.shape, lambda i: (0, 0))],
        out_specs=row(d),
        out_shape=jax.ShapeDtypeStruct((s, d), F32),
        compiler_params=_params(("parallel",)),
        name="proj_residual",
    )(a, h, w)


def _mixer_kernel(zc_ref, zh_ref, h_ref, cw_ref, pw_ref, ps_ref, wo_ref, o_ref, y_ref, *, tm, cdim, pgroup):
    i = pl.program_id(0)
    has_prev = i > 0
    chunk = pgroup

    def ext(col0):
        sl = slice(col0, col0 + chunk)
        halo = jnp.where(has_prev, zh_ref[:, sl].astype(F32), 0.0)
        return jnp.concatenate([halo, zc_ref[:, sl].astype(F32)], axis=0)

    for c0 in range(0, cdim, chunk):
        cx = ext(cdim + c0) * ext(2 * cdim + c0)
        w = cw_ref[:, c0:c0 + chunk]
        conv = (w[2:3] * cx + w[1:2] * pltpu.roll(cx, 1, 0) + w[0:1] * pltpu.roll(cx, 2, 0))[HALO:]
        gate_b = zc_ref[:, c0:c0 + chunk].astype(F32)
        y_ref[:, c0:c0 + chunk] = (gate_b * conv).astype(BF16)

    t = i * tm + lax.broadcasted_iota(jnp.int32, (tm, 1), 0)
    for gi, win in enumerate(POOL_WINDOWS):
        col0 = 3 * cdim + gi * chunk
        acc = ext(col0)
        sh = 1
        while sh < win:
            acc = acc + pltpu.roll(acc, sh, 0)
            sh *= 2
        inv_cnt = 1.0 / jnp.minimum(t + 1, win).astype(F32)
        u = zc_ref[:, col0:col0 + chunk].astype(F32)
        pooled = acc[HALO:] * inv_cnt - u
        mixed = _dot(pooled.astype(BF16), pw_ref[gi])
        y_ref[:, cdim + gi * chunk: cdim + (gi + 1) * chunk] = (
            mixed * ps_ref[:, gi * chunk:(gi + 1) * chunk]).astype(BF16)

    o_ref[...] = h_ref[...] + _dot(y_ref[...], wo_ref[...])


def _mixer(z, h, conv_w, pool_w, pool_scale, w_out, *, tm=512):
    s, d = h.shape
    cdim = d // 2
    pgroup = pool_w.shape[-1]
    tm = _pick(s, tm)
    assert tm % HALO == 0 and cdim % pgroup == 0 and max(POOL_WINDOWS) <= HALO
    hb = tm // HALO
    return pl.pallas_call(
        functools.partial(_mixer_kernel, tm=tm, cdim=cdim, pgroup=pgroup),
        grid=(s // tm,),
        in_specs=[
            pl.BlockSpec((tm, 4 * cdim), lambda i: (i, 0)),
            pl.BlockSpec((HALO, 4 * cdim), lambda i: (jnp.maximum(i * hb - 1, 0), 0)),
            pl.BlockSpec((tm, d), lambda i: (i, 0)),
            pl.BlockSpec((CONV_WIDTH, cdim), lambda i: (0, 0)),
            pl.BlockSpec(pool_w.shape, lambda i: (0, 0, 0)),
            pl.BlockSpec((1, cdim), lambda i: (0, 0)),
            pl.BlockSpec((d, d), lambda i: (0, 0)),
        ],
        out_specs=pl.BlockSpec((tm, d), lambda i: (i, 0)),
        out_shape=jax.ShapeDtypeStruct((s, d), F32),
        scratch_shapes=[pltpu.VMEM((tm, d), BF16)],
        compiler_params=_params(("parallel",)),
        name="mixer",
    )(z, z, h, conv_w, pool_w, pool_scale.reshape(1, cdim), w_out)


def _attn_kernel(q_ref, k_ref, v_ref, o_ref, *scr, dils, span):
    n_br = len(dils)
    ks, vs, qs, ogs, lgs = (scr[i * n_br:(i + 1) * n_br] for i in range(5))
    t = pl.program_id(1)
    blk = ATT_BLOCK
    tile = q_ref.shape[0]

    @pl.when(t == 0)
    def _():
        for kd, vd in zip(ks, vs):
            kd[:, :blk, :] = jnp.zeros((kd.shape[0], blk, HEAD_DIM), BF16)
            vd[:, :blk, :] = jnp.zeros((vd.shape[0], blk, HEAD_DIM), BF16)

    @pl.when(t > 0)
    def _():
        for kd, vd, d in zip(ks, vs, dils):
            n = tile // d
            kd[:, :blk, :] = kd[:, n:n + blk, :]
            vd[:, :blk, :] = vd[:, n:n + blk, :]

    for kd, vd, qd, d in zip(ks, vs, qs, dils):
        n = tile // d
        for r in range(d):
            rows = pl.ds(r, n, stride=d) if d > 1 else slice(None)
            kd[r, blk:, :] = k_ref[rows, :].astype(BF16)
            vd[r, blk:, :] = v_ref[rows, :].astype(BF16)
            qd[r] = q_ref[rows, :].astype(BF16)

    a_i = lax.broadcasted_iota(jnp.int32, (blk, 2 * blk), 0)
    c_i = lax.broadcasted_iota(jnp.int32, (blk, 2 * blk), 1)
    dist = a_i + blk - c_i
    band = (dist >= 0) & (dist <= span)
    band_first = band & (c_i >= jnp.where(t == 0, blk, 0))

    for g, d in enumerate(dils):
        n = tile // d
        for r in range(d):
            for b in range(n // blk):
                q = qs[g][r, b * blk:(b + 1) * blk, :]
                kk = ks[g][r, b * blk:(b + 2) * blk, :]
                vv = vs[g][r, b * blk:(b + 2) * blk, :]
                s = lax.dot_general(q, kk, (((1,), (1,)), ((), ())), preferred_element_type=F32)
                s = jnp.where(band_first if b == 0 else band, s, NEG_INF)
                m = jnp.max(s, axis=-1, keepdims=True)
                p = jnp.exp2(s - m)
                l = jnp.sum(p, axis=-1, keepdims=True)
                o = _dot(p.astype(BF16), vv) / l
                lse = m * LN2 + jnp.log(l)
                rows = pl.ds(b * blk * d + r, blk, stride=d) if d > 1 else slice(b * blk, (b + 1) * blk)
                ogs[g][rows, :] = o
                lgs[g][rows, :] = jnp.broadcast_to(lse, (blk, HEAD_DIM))

    for c0 in range(0, tile, COMBINE_ROWS):
        rows = slice(c0, c0 + COMBINE_ROWS)
        ls = [lg[rows, :] for lg in lgs]
        m = functools.reduce(jnp.maximum, ls)
        es = [jnp.exp(x - m) for x in ls]
        tot = functools.reduce(lambda x, y: x + y, es)
        acc = functools.reduce(lambda x, y: x + y, [e * og[rows, :] for e, og in zip(es, ogs)])
        o_ref[rows, :] = (acc / tot).astype(o_ref.dtype)


def _attention(qkv, *, tile=ATT_TILE):
    s, c3 = qkv.shape
    d_model = c3 // 3
    n_heads = d_model // HEAD_DIM
    dils = tuple(d for _, d in DIL_BRANCHES)
    spans = {w // d for w, d in DIL_BRANCHES}
    assert len(spans) == 1 and s % tile == 0 and all(tile % (ATT_BLOCK * d) == 0 for d in dils)
    span = spans.pop()
    assert span <= ATT_BLOCK
    kv = [pltpu.VMEM((d, ATT_BLOCK + tile // d, HEAD_DIM), BF16) for d in dils]
    qd = [pltpu.VMEM((d, tile // d, HEAD_DIM), BF16) for d in dils]
    acc = [pltpu.VMEM((tile, HEAD_DIM), F32) for _ in dils]
    blockspec = lambda part: pl.BlockSpec((tile, HEAD_DIM), lambda h, t: (t, part * n_heads + h))
    return pl.pallas_call(
        functools.partial(_attn_kernel, dils=dils, span=span),
        grid=(n_heads, s // tile),
        in_specs=[blockspec(0), blockspec(1), blockspec(2)],
        out_specs=pl.BlockSpec((tile, HEAD_DIM), lambda h, t: (t, h)),
        out_shape=jax.ShapeDtypeStruct((s, d_model), BF16),
        scratch_shapes=kv + kv + qd + acc + acc,
        compiler_params=_params(("parallel", "arbitrary")),
        name="attn",
    )(qkv, qkv, qkv)


def kernel(x, positions, norm_g, ffn_w1, ffn_w3, ffn_w2, mix_w_in, conv_w, pool_w, pool_scale,
           mix_w_out, attn_w_qkv, attn_w_o, final_g):
    b, s, d = x.shape
    depth = norm_g.shape[0]
    bf = lambda w: w.astype(BF16)
    ffns = [(i, j) for i in range(depth) for j in range(2)]
    wts = {ffns[0]: [bf(w[ffns[0]]) for w in (ffn_w1, ffn_w3, ffn_w2)]}

    def ffn(h, key, g, final_g=None):
        nxt = ffns.index(key) + 1
        if nxt < len(ffns) and ffns[nxt] not in wts:
            h, *wts[ffns[nxt]] = _ffn(h, g, *wts[key], final_g=final_g,
                                      cast=[(w, ffns[nxt]) for w in (ffn_w1, ffn_w3, ffn_w2)])
            return h
        return _ffn(h, g, *wts[key], final_g=final_g)

    outs = []
    for bi in range(b):
        h = x[bi]
        for i in range(depth):
            h = ffn(h, (i, 0), norm_g[i, 0])
            j = i // 2
            if i % 2 == 0:
                z = _norm_matmul(h, norm_g[i, 1], bf(mix_w_in[j]), tn=1024)
                h = _mixer(z, h, conv_w[j], bf(pool_w[j]), pool_scale[j], bf(mix_w_out[j]))
            else:
                qkv = _norm_matmul(h, norm_g[i, 1], bf(attn_w_qkv[j]), positions[bi], out_dtype=F32, tn=1024)
                h = _proj_residual(_attention(qkv), h, bf(attn_w_o[j]))
            h = ffn(h, (i, 1), norm_g[i, 2], final_g=final_g if i == depth - 1 else None)
        outs.append(h.reshape(1, s, d))
    return outs[0] if b == 1 else jnp.concatenate(outs, axis=0)
```

```python
import functools

import numpy as np
import jax
import jax.numpy as jnp
from jax import lax
from jax.experimental import pallas as pl
from jax.experimental.pallas import tpu as pltpu

F32 = jnp.float32
BF16 = jnp.bfloat16

HEAD_DIM = 128
ROPE_DIM = HEAD_DIM // 4
ROPE_THETA = 500000.0
DIL_BRANCHES = ((128, 1), (512, 4), (2048, 16))
ATT_BLOCK = 128
ATT_TILE = ATT_BLOCK * max(d for _, d in DIL_BRANCHES)
COMBINE_ROWS = 256
EPILOGUE_ROWS = 256
CONV_WIDTH = 3
POOL_WINDOWS = (2, 4, 8, 16)
HALO = 16
RMS_EPS = 1e-6
NEG_INF = -1e30
LOG2E = float(np.log2(np.e))
LN2 = float(np.log(2.0))

MXU_COLS = 256
V7X_VMEM_BYTES = 64 * 1024 * 1024
VMEM_LIMIT = V7X_VMEM_BYTES - 6 * 1024 * 1024


def _params(sem):
    return pltpu.CompilerParams(dimension_semantics=sem, vmem_limit_bytes=VMEM_LIMIT)


def _rms(x, g):
    ms = jnp.mean(x * x, axis=-1, keepdims=True)
    return x * lax.rsqrt(ms + RMS_EPS) * g


def _dot(a, b):
    return jnp.dot(a, b, preferred_element_type=F32)


def _pick(n, pref):
    t = min(n, pref)
    while n % t:
        t //= 2
    return t


def _ffn_kernel(x_ref, g_ref, w1_ref, w3_ref, w2_ref, *rest, n_f, final_norm, n_cast):
    rest = list(rest)
    fg_ref = rest.pop(0) if final_norm else None
    cast_in = [rest.pop(0) for _ in range(n_cast)]
    o_ref = rest.pop(0)
    cast_out = [rest.pop(0) for _ in range(n_cast)]
    (xn_ref,) = rest
    f = pl.program_id(1)

    def step():
        xn = xn_ref[...]
        h1 = _dot(xn, w1_ref[...])
        h3 = _dot(xn, w3_ref[...])
        a = (h1 * jax.nn.sigmoid(h1) * (0.5 * h3)).astype(BF16)
        o_ref[...] += _dot(a, w2_ref[...])
        for src, dst in zip(cast_in, cast_out):
            dst[...] = src[...].astype(BF16)

    @pl.when(f == 0)
    def _():
        xn_ref[...] = _rms(x_ref[...], g_ref[...]).astype(BF16)
        o_ref[...] = x_ref[...]
        step()

    @pl.when(f > 0)
    def _():
        step()

    if final_norm:
        @pl.when(f == n_f - 1)
        def _():
            def chunk(c, carry):
                rows = pl.ds(pl.multiple_of(c * EPILOGUE_ROWS, EPILOGUE_ROWS), EPILOGUE_ROWS)
                o_ref[rows, :] = _rms(o_ref[rows, :], fg_ref[...])
                return carry

            lax.fori_loop(0, o_ref.shape[0] // EPILOGUE_ROWS, chunk, 0)


def _cast_specs(w, lead, n_i, n_f):
    r, c = w.shape[-2:]
    wide = c >= r
    shape = (r // n_i, c // n_f) if wide else (r // n_f, c // n_i)
    assert shape[0] * (n_i if wide else n_f) == r and shape[1] * (n_f if wide else n_i) == c
    assert shape[0] % 16 == 0 and shape[1] % 128 == 0, shape
    block = (lambda i, f: (i, f)) if wide else (lambda i, f: (f, i))
    src = pl.BlockSpec((None,) * len(lead) + shape, lambda i, f: tuple(lead) + block(i, f))
    return src, pl.BlockSpec(shape, block)


def _ffn(x, g, w1, w3, w2, final_g=None, cast=(), *, tm=1024, tf=512):
    s, d = x.shape
    ff = w1.shape[-1]
    tm, tf = _pick(s, tm), _pick(ff, tf)
    n_i, n_f = s // tm, ff // tf
    final_norm = final_g is not None
    in_specs = [
        pl.BlockSpec((tm, d), lambda i, f: (i, 0)),
        pl.BlockSpec((1, d), lambda i, f: (0, 0)),
        pl.BlockSpec((d, tf), lambda i, f: (0, f)),
        pl.BlockSpec((d, tf), lambda i, f: (0, f)),
        pl.BlockSpec((tf, d), lambda i, f: (f, 0)),
    ]
    args = [x, g.reshape(1, d), w1, w3, w2]
    if final_norm:
        in_specs.append(pl.BlockSpec((1, d), lambda i, f: (0, 0)))
        args.append(final_g.reshape(1, d))
    cast_specs = [_cast_specs(w, lead, n_i, n_f) for w, lead in cast]
    outs = pl.pallas_call(
        functools.partial(_ffn_kernel, n_f=n_f, final_norm=final_norm, n_cast=len(cast)),
        grid=(n_i, n_f),
        in_specs=in_specs + [src for src, _ in cast_specs],
        out_specs=[pl.BlockSpec((tm, d), lambda i, f: (i, 0))] + [dst for _, dst in cast_specs],
        out_shape=[jax.ShapeDtypeStruct((s, d), F32)]
        + [jax.ShapeDtypeStruct(w.shape[-2:], BF16) for w, _ in cast],
        scratch_shapes=[pltpu.VMEM((tm, d), BF16)],
        compiler_params=_params(("parallel", "arbitrary")),
        name="ffn",
    )(*args, *[w for w, _ in cast])
    return outs[0] if not cast else outs


def _norm_matmul_kernel(x_ref, g_ref, w_ref, *rest, rope, tiles_per_part, heads_per_tile, q_scale, n_cast):
    rest = list(rest)
    pos_ref, inv_ref = (rest.pop(0), rest.pop(0)) if rope else (None, None)
    cast_in = [rest.pop(0) for _ in range(n_cast)]
    o_ref = rest.pop(0)
    cast_out = [rest.pop(0) for _ in range(n_cast)]
    xn_ref = rest.pop(0)
    tab_ref = rest.pop(0) if rope else None
    j = pl.program_id(1)

    def prologue():
        xn_ref[...] = _rms(x_ref[...], g_ref[...]).astype(BF16)
        if rope:
            ang = pos_ref[...].astype(F32) * inv_ref[...]
            lane = lax.broadcasted_iota(jnp.int32, ang.shape, 1)
            sn = jnp.sin(ang)
            c = jnp.cos(ang)
            a = jnp.where(lane < ROPE_DIM // 2, -sn, 0.0)
            b = jnp.where(lane >= ROPE_DIM // 2, sn, 0.0)
            for idx, tab in enumerate((c, a, b)):
                tab_ref[0, idx] = tab * q_scale
                tab_ref[1, idx] = tab
            tab_ref[2, 0] = jnp.ones(ang.shape, F32)
            tab_ref[2, 1] = jnp.zeros(ang.shape, F32)
            tab_ref[2, 2] = jnp.zeros(ang.shape, F32)

    def step():
        for src, dst in zip(cast_in, cast_out):
            dst[...] = src[...].astype(BF16)
        if not rope:
            o_ref[...] = _dot(xn_ref[...], w_ref[...]).astype(o_ref.dtype)
            return
        part = j // tiles_per_part
        for c0 in range(0, heads_per_tile * HEAD_DIM, MXU_COLS):
            y = _dot(xn_ref[...], w_ref[:, c0:c0 + MXU_COLS])
            for h0 in range(0, MXU_COLS, HEAD_DIM):
                yh = y[:, h0:h0 + HEAD_DIM]
                up = pltpu.roll(yh, HEAD_DIM - ROPE_DIM // 2, 1)
                dn = pltpu.roll(yh, ROPE_DIM // 2, 1)
                out = yh * tab_ref[part, 0] + up * tab_ref[part, 1] + dn * tab_ref[part, 2]
                o_ref[:, c0 + h0:c0 + h0 + HEAD_DIM] = out.astype(o_ref.dtype)

    @pl.when(j == 0)
    def _():
        prologue()
        step()

    @pl.when(j > 0)
    def _():
        step()


def _rope_inv_freq():
    inv = np.zeros((1, HEAD_DIM), np.float32)
    freqs = ROPE_THETA ** (-np.arange(0, ROPE_DIM, 2, dtype=np.float64) / ROPE_DIM)
    inv[0, :ROPE_DIM // 2] = freqs
    inv[0, ROPE_DIM // 2:ROPE_DIM] = freqs
    return jnp.asarray(inv)


def _norm_matmul(x, g, w, positions=None, cast=(), *, out_dtype=BF16, tm=1024, tn=512):
    s, d = x.shape
    n = w.shape[1]
    tm, tn = _pick(s, tm), _pick(n, tn)
    rope = positions is not None
    cast_specs = [_cast_specs(cw, lead, s // tm, n // tn) for cw, lead in cast]
    in_specs = [
        pl.BlockSpec((tm, d), lambda i, j: (i, 0)),
        pl.BlockSpec((1, d), lambda i, j: (0, 0)),
        pl.BlockSpec((d, tn), lambda i, j: (0, j)),
    ]
    args = [x, g.reshape(1, d), w]
    scratch = [pltpu.VMEM((tm, d), BF16)]
    tiles_per_part = 0
    if rope:
        assert n % 3 == 0 and (n // 3) % tn == 0 and tn % MXU_COLS == 0 and MXU_COLS % HEAD_DIM == 0
        tiles_per_part = n // 3 // tn
        in_specs += [pl.BlockSpec((tm, 1), lambda i, j: (i, 0)),
                     pl.BlockSpec((1, HEAD_DIM), lambda i, j: (0, 0))]
        args += [positions.reshape(s, 1), _rope_inv_freq()]
        scratch.append(pltpu.VMEM((3, 3, tm, HEAD_DIM), F32))
    outs = pl.pallas_call(
        functools.partial(_norm_matmul_kernel, rope=rope, tiles_per_part=tiles_per_part,
                          heads_per_tile=tn // HEAD_DIM, q_scale=LOG2E / np.sqrt(HEAD_DIM),
                          n_cast=len(cast)),
        grid=(s // tm, n // tn),
        in_specs=in_specs + [src for src, _ in cast_specs],
        out_specs=[pl.BlockSpec((tm, tn), lambda i, j: (i, j))] + [dst for _, dst in cast_specs],
        out_shape=[jax.ShapeDtypeStruct((s, n), out_dtype)]
        + [jax.ShapeDtypeStruct(cw.shape[-2:], BF16) for cw, _ in cast],
        scratch_shapes=scratch,
        compiler_params=_params(("parallel", "arbitrary")),
        name="norm_matmul_rope" if rope else "norm_matmul",
    )(*args, *[cw for cw, _ in cast])
    return outs[0] if not cast else outs


def _proj_residual_kernel(a_ref, h_ref, w_ref, o_ref):
    o_ref[...] = h_ref[...] + _dot(a_ref[...], w_ref[...])


def _proj_residual(a, h, w, *, tm=512):
    s, d = h.shape
    tm = _pick(s, tm)
    row = lambda width: pl.BlockSpec((tm, width), lambda i: (i, 0))
    return pl.pallas_call(
        _proj_residual_kernel,
        grid=(s // tm,),
        in_specs=[row(a.shape[1]), row(d), pl.BlockSpec(w.shape, lambda i: (0, 0))],
        out_specs=row(d),
        out_shape=jax.ShapeDtypeStruct((s, d), F32),
        compiler_params=_params(("parallel",)),
        name="proj_residual",
    )(a, h, w)


def _mixer_kernel(zc_ref, zh_ref, h_ref, cw_ref, pw_ref, ps_ref, wo_ref, o_ref, y_ref, *, tm, cdim, pgroup):
    i = pl.program_id(0)
    has_prev = i > 0
    chunk = pgroup

    def ext(col0):
        sl = slice(col0, col0 + chunk)
        halo = jnp.where(has_prev, zh_ref[:, sl].astype(F32), 0.0)
        return jnp.concatenate([halo, zc_ref[:, sl].astype(F32)], axis=0)

    for c0 in range(0, cdim, chunk):
        cx = ext(cdim + c0) * ext(2 * cdim + c0)
        w = cw_ref[:, c0:c0 + chunk]
        conv = (w[2:3] * cx + w[1:2] * pltpu.roll(cx, 1, 0) + w[0:1] * pltpu.roll(cx, 2, 0))[HALO:]
        gate_b = zc_ref[:, c0:c0 + chunk].astype(F32)
        y_ref[:, c0:c0 + chunk] = (gate_b * conv).astype(BF16)

    o_ref[...] = h_ref[...] + _dot(y_ref[:, :cdim], wo_ref[:cdim, :])

    t = i * tm + lax.broadcasted_iota(jnp.int32, (tm, 1), 0)
    for gi, win in enumerate(POOL_WINDOWS):
        col0 = 3 * cdim + gi * chunk
        acc = ext(col0)
        sh = 1
        while sh < win:
            acc = acc + pltpu.roll(acc, sh, 0)
            sh *= 2
        inv_cnt = 1.0 / jnp.minimum(t + 1, win).astype(F32)
        u = zc_ref[:, col0:col0 + chunk].astype(F32)
        pooled = acc[HALO:] * inv_cnt - u
        mixed = _dot(pooled.astype(BF16), pw_ref[gi])
        y_ref[:, cdim + gi * chunk: cdim + (gi + 1) * chunk] = (
            mixed * ps_ref[:, gi * chunk:(gi + 1) * chunk]).astype(BF16)

    o_ref[...] += _dot(y_ref[:, cdim:], wo_ref[cdim:, :])


def _mixer(z, h, conv_w, pool_w, pool_scale, w_out, *, tm=512):
    s, d = h.shape
    cdim = d // 2
    pgroup = pool_w.shape[-1]
    tm = _pick(s, tm)
    assert tm % HALO == 0 and cdim % pgroup == 0 and max(POOL_WINDOWS) <= HALO
    hb = tm // HALO
    return pl.pallas_call(
        functools.partial(_mixer_kernel, tm=tm, cdim=cdim, pgroup=pgroup),
        grid=(s // tm,),
        in_specs=[
            pl.BlockSpec((tm, 4 * cdim), lambda i: (i, 0)),
            pl.BlockSpec((HALO, 4 * cdim), lambda i: (jnp.maximum(i * hb - 1, 0), 0)),
            pl.BlockSpec((tm, d), lambda i: (i, 0)),
            pl.BlockSpec((CONV_WIDTH, cdim), lambda i: (0, 0)),
            pl.BlockSpec(pool_w.shape, lambda i: (0, 0, 0)),
            pl.BlockSpec((1, cdim), lambda i: (0, 0)),
            pl.BlockSpec((d, d), lambda i: (0, 0)),
        ],
        out_specs=pl.BlockSpec((tm, d), lambda i: (i, 0)),
        out_shape=jax.ShapeDtypeStruct((s, d), F32),
        scratch_shapes=[pltpu.VMEM((tm, d), BF16)],
        compiler_params=_params(("parallel",)),
        name="mixer",
    )(z, z, h, conv_w, pool_w, pool_scale.reshape(1, cdim), w_out)


def _attn_kernel(q_ref, k_ref, v_ref, o_ref, *scr, dils, span):
    n_br = len(dils)
    ks, vs, qs, ogs, lgs = (scr[i * n_br:(i + 1) * n_br] for i in range(5))
    t = pl.program_id(1)
    blk = ATT_BLOCK
    tile = q_ref.shape[0]

    @pl.when(t == 0)
    def _():
        for kd, vd in zip(ks, vs):
            kd[:, :blk, :] = jnp.zeros((kd.shape[0], blk, HEAD_DIM), BF16)
            vd[:, :blk, :] = jnp.zeros((vd.shape[0], blk, HEAD_DIM), BF16)

    @pl.when(t > 0)
    def _():
        for kd, vd, d in zip(ks, vs, dils):
            n = tile // d
            kd[:, :blk, :] = kd[:, n:n + blk, :]
            vd[:, :blk, :] = vd[:, n:n + blk, :]

    for kd, vd, qd, d in zip(ks, vs, qs, dils):
        n = tile // d
        for r in range(d):
            rows = pl.ds(r, n, stride=d) if d > 1 else slice(None)
            kd[r, blk:, :] = k_ref[rows, :].astype(BF16)
            vd[r, blk:, :] = v_ref[rows, :].astype(BF16)
            qd[r] = q_ref[rows, :].astype(BF16)

    a_i = lax.broadcasted_iota(jnp.int32, (blk, 2 * blk), 0)
    c_i = lax.broadcasted_iota(jnp.int32, (blk, 2 * blk), 1)
    dist = a_i + blk - c_i
    band = (dist >= 0) & (dist <= span)
    band_first = band & (c_i >= jnp.where(t == 0, blk, 0))

    for g, d in enumerate(dils):
        n = tile // d
        for r in range(d):
            for b in range(n // blk):
                q = qs[g][r, b * blk:(b + 1) * blk, :]
                kk = ks[g][r, b * blk:(b + 2) * blk, :]
                vv = vs[g][r, b * blk:(b + 2) * blk, :]
                s = lax.dot_general(q, kk, (((1,), (1,)), ((), ())), preferred_element_type=F32)
                s = jnp.where(band_first if b == 0 else band, s, NEG_INF)
                m = jnp.max(s, axis=-1, keepdims=True)
                p = jnp.exp2(s - m)
                l = jnp.sum(p, axis=-1, keepdims=True)
                o = _dot(p.astype(BF16), vv) / l
                lse = m * LN2 + jnp.log(l)
                rows = pl.ds(b * blk * d + r, blk, stride=d) if d > 1 else slice(b * blk, (b + 1) * blk)
                ogs[g][rows, :] = o
                lgs[g][rows, :] = jnp.broadcast_to(lse, (blk, HEAD_DIM))

    for c0 in range(0, tile, COMBINE_ROWS):
        rows = slice(c0, c0 + COMBINE_ROWS)
        ls = [lg[rows, :] for lg in lgs]
        m = functools.reduce(jnp.maximum, ls)
        es = [jnp.exp(x - m) for x in ls]
        tot = functools.reduce(lambda x, y: x + y, es)
        acc = functools.reduce(lambda x, y: x + y, [e * og[rows, :] for e, og in zip(es, ogs)])
        o_ref[rows, :] = (acc / tot).astype(o_ref.dtype)


def _attention(qkv, *, tile=ATT_TILE):
    s, c3 = qkv.shape
    d_model = c3 // 3
    n_heads = d_model // HEAD_DIM
    dils = tuple(d for _, d in DIL_BRANCHES)
    spans = {w // d for w, d in DIL_BRANCHES}
    assert len(spans) == 1 and s % tile == 0 and all(tile % (ATT_BLOCK * d) == 0 for d in dils)
    span = spans.pop()
    assert span <= ATT_BLOCK
    kv = [pltpu.VMEM((d, ATT_BLOCK + tile // d, HEAD_DIM), BF16) for d in dils]
    qd = [pltpu.VMEM((d, tile // d, HEAD_DIM), BF16) for d in dils]
    acc = [pltpu.VMEM((tile, HEAD_DIM), F32) for _ in dils]
    blockspec = lambda part: pl.BlockSpec((tile, HEAD_DIM), lambda h, t: (t, part * n_heads + h))
    return pl.pallas_call(
        functools.partial(_attn_kernel, dils=dils, span=span),
        grid=(n_heads, s // tile),
        in_specs=[blockspec(0), blockspec(1), blockspec(2)],
        out_specs=pl.BlockSpec((tile, HEAD_DIM), lambda h, t: (t, h)),
        out_shape=jax.ShapeDtypeStruct((s, d_model), BF16),
        scratch_shapes=kv + kv + qd + acc + acc,
        compiler_params=_params(("parallel", "arbitrary")),
        name="attn",
    )(qkv, qkv, qkv)


def kernel(x, positions, norm_g, ffn_w1, ffn_w3, ffn_w2, mix_w_in, conv_w, pool_w, pool_scale,
           mix_w_out, attn_w_qkv, attn_w_o, final_g):
    b, s, d = x.shape
    depth = norm_g.shape[0]
    bf = lambda w: w.astype(BF16)
    ffns = [(i, j) for i in range(depth) for j in range(2)]
    wts = {ffns[0]: [bf(w[ffns[0]]) for w in (ffn_w1, ffn_w3, ffn_w2)]}

    def ffn(h, key, g, final_g=None):
        nxt = ffns.index(key) + 1
        if nxt < len(ffns) and ffns[nxt] not in wts:
            h, *wts[ffns[nxt]] = _ffn(h, g, *wts[key], final_g=final_g,
                                      cast=[(w, ffns[nxt]) for w in (ffn_w1, ffn_w3, ffn_w2)])
            return h
        return _ffn(h, g, *wts[key], final_g=final_g)

    proj_w = {}
    outs = []
    for bi in range(b):
        h = x[bi]
        for i in range(depth):
            h = ffn(h, (i, 0), norm_g[i, 0])
            j = i // 2
            if i % 2 == 0:
                later = {"mix_out": (mix_w_out, (j,))}
                if i + 1 < depth:
                    later.update(qkv=(attn_w_qkv, (j,)), attn_out=(attn_w_o, (j,)))
                later = {k: v for k, v in later.items() if (k, v[1]) not in proj_w}
                res = _norm_matmul(h, norm_g[i, 1], bf(mix_w_in[j]), cast=list(later.values()), tn=1024)
                z, cast_w = (res[0], res[1:]) if later else (res, [])
                proj_w.update({(k, v[1]): cw for (k, v), cw in zip(later.items(), cast_w)})
                h = _mixer(z, h, conv_w[j], bf(pool_w[j]), pool_scale[j], proj_w["mix_out", (j,)])
            else:
                w_qkv = proj_w.get(("qkv", (j,)))
                w_o = proj_w.get(("attn_out", (j,)))
                w_qkv = bf(attn_w_qkv[j]) if w_qkv is None else w_qkv
                w_o = bf(attn_w_o[j]) if w_o is None else w_o
                qkv = _norm_matmul(h, norm_g[i, 1], w_qkv, positions[bi], out_dtype=F32, tn=1024)
                h = _proj_residual(_attention(qkv), h, w_o)
            h = ffn(h, (i, 1), norm_g[i, 2], final_g=final_g if i == depth - 1 else None)
        outs.append(h.reshape(1, s, d))
    return outs[0] if b == 1 else jnp.concatenate(outs, axis=0)
```

```python
import functools

import numpy as np
import jax
import jax.numpy as jnp
from jax import lax
from jax.experimental import pallas as pl
from jax.experimental.pallas import tpu as pltpu

F32 = jnp.float32
BF16 = jnp.bfloat16

HEAD_DIM = 128
ROPE_DIM = HEAD_DIM // 4
ROPE_THETA = 500000.0
DIL_BRANCHES = ((128, 1), (512, 4), (2048, 16))
ATT_BLOCK = 128
ATT_TILE = ATT_BLOCK * max(d for _, d in DIL_BRANCHES)
COMBINE_ROWS = 256
EPILOGUE_ROWS = 256
CONV_WIDTH = 3
POOL_WINDOWS = (2, 4, 8, 16)
HALO = 16
RMS_EPS = 1e-6
NEG_INF = -1e30
LOG2E = float(np.log2(np.e))
LN2 = float(np.log(2.0))

MXU_COLS = 256
ROPE_TILE = 512
V7X_VMEM_BYTES = 64 * 1024 * 1024
VMEM_LIMIT = V7X_VMEM_BYTES - 6 * 1024 * 1024


def _params(sem):
    return pltpu.CompilerParams(dimension_semantics=sem, vmem_limit_bytes=VMEM_LIMIT)


def _rms(x, g):
    ms = jnp.mean(x * x, axis=-1, keepdims=True)
    return x * lax.rsqrt(ms + RMS_EPS) * g


def _dot(a, b):
    return jnp.dot(a, b, preferred_element_type=F32)


def _pick(n, pref):
    t = min(n, pref)
    while n % t:
        t //= 2
    return t


def _ffn_kernel(x_ref, g_ref, w1_ref, w3_ref, w2_ref, *rest, n_f, final_norm, n_cast):
    rest = list(rest)
    fg_ref = rest.pop(0) if final_norm else None
    cast_in = [rest.pop(0) for _ in range(n_cast)]
    o_ref = rest.pop(0)
    cast_out = [rest.pop(0) for _ in range(n_cast)]
    (xn_ref,) = rest
    f = pl.program_id(1)

    def step():
        xn = xn_ref[...]
        h1 = _dot(xn, w1_ref[...])
        h3 = _dot(xn, w3_ref[...])
        a = (h1 * jax.nn.sigmoid(h1) * (0.5 * h3)).astype(BF16)
        o_ref[...] += _dot(a, w2_ref[...])
        for src, dst in zip(cast_in, cast_out):
            dst[...] = src[...].astype(BF16)

    @pl.when(f == 0)
    def _():
        xn_ref[...] = _rms(x_ref[...], g_ref[...]).astype(BF16)
        o_ref[...] = x_ref[...]
        step()

    @pl.when(f > 0)
    def _():
        step()

    if final_norm:
        @pl.when(f == n_f - 1)
        def _():
            def chunk(c, carry):
                rows = pl.ds(pl.multiple_of(c * EPILOGUE_ROWS, EPILOGUE_ROWS), EPILOGUE_ROWS)
                o_ref[rows, :] = _rms(o_ref[rows, :], fg_ref[...])
                return carry

            lax.fori_loop(0, o_ref.shape[0] // EPILOGUE_ROWS, chunk, 0)


def _cast_specs(w, lead, n_i, n_f):
    r, c = w.shape[-2:]
    wide = c >= r
    shape = (r // n_i, c // n_f) if wide else (r // n_f, c // n_i)
    assert shape[0] * (n_i if wide else n_f) == r and shape[1] * (n_f if wide else n_i) == c
    assert shape[0] % 16 == 0 and shape[1] % 128 == 0, shape
    block = (lambda i, f: (i, f)) if wide else (lambda i, f: (f, i))
    src = pl.BlockSpec((None,) * len(lead) + shape, lambda i, f: tuple(lead) + block(i, f))
    return src, pl.BlockSpec(shape, block)


def _ffn(x, g, w1, w3, w2, final_g=None, cast=(), *, tm=1024, tf=512):
    s, d = x.shape
    ff = w1.shape[-1]
    tm, tf = _pick(s, tm), _pick(ff, tf)
    n_i, n_f = s // tm, ff // tf
    final_norm = final_g is not None
    in_specs = [
        pl.BlockSpec((tm, d), lambda i, f: (i, 0)),
        pl.BlockSpec((1, d), lambda i, f: (0, 0)),
        pl.BlockSpec((d, tf), lambda i, f: (0, f)),
        pl.BlockSpec((d, tf), lambda i, f: (0, f)),
        pl.BlockSpec((tf, d), lambda i, f: (f, 0)),
    ]
    args = [x, g.reshape(1, d), w1, w3, w2]
    if final_norm:
        in_specs.append(pl.BlockSpec((1, d), lambda i, f: (0, 0)))
        args.append(final_g.reshape(1, d))
    cast_specs = [_cast_specs(w, lead, n_i, n_f) for w, lead in cast]
    outs = pl.pallas_call(
        functools.partial(_ffn_kernel, n_f=n_f, final_norm=final_norm, n_cast=len(cast)),
        grid=(n_i, n_f),
        in_specs=in_specs + [src for src, _ in cast_specs],
        out_specs=[pl.BlockSpec((tm, d), lambda i, f: (i, 0))] + [dst for _, dst in cast_specs],
        out_shape=[jax.ShapeDtypeStruct((s, d), F32)]
        + [jax.ShapeDtypeStruct(w.shape[-2:], BF16) for w, _ in cast],
        scratch_shapes=[pltpu.VMEM((tm, d), BF16)],
        compiler_params=_params(("parallel", "arbitrary")),
        name="ffn",
    )(*args, *[w for w, _ in cast])
    return outs[0] if not cast else outs


def _norm_matmul_kernel(x_ref, g_ref, w_ref, *rest, n_cast):
    cast_in, (o_ref, *cast_out), xn_ref = rest[:n_cast], rest[n_cast:2 * n_cast + 1], rest[-1]
    j = pl.program_id(1)

    def step():
        for src, dst in zip(cast_in, cast_out):
            dst[...] = src[...].astype(BF16)
        o_ref[...] = _dot(xn_ref[...], w_ref[...]).astype(o_ref.dtype)

    @pl.when(j == 0)
    def _():
        xn_ref[...] = _rms(x_ref[...], g_ref[...]).astype(BF16)
        step()

    @pl.when(j > 0)
    def _():
        step()


def _qkv_proj_kernel(x_ref, g_ref, wr_ref, wv_ref, pos_ref, inv_ref, qk_ref, v_ref, xn_ref, tab_ref,
                     *, tiles_per_part, q_scale):
    j = pl.program_id(1)

    def prologue():
        xn_ref[...] = _rms(x_ref[...], g_ref[...]).astype(BF16)
        half = ROPE_DIM // 2
        ang = inv_ref[...] * pos_ref[...].astype(F32)
        cs, sn = jnp.cos(ang), jnp.sin(ang)
        zeros = lambda n: jnp.zeros((n, ang.shape[1]), F32)
        c = jnp.concatenate([cs, cs, jnp.ones((HEAD_DIM - 2 * half, ang.shape[1]), F32)], axis=0).T
        a = jnp.concatenate([-sn, zeros(HEAD_DIM - half)], axis=0).T
        b = jnp.concatenate([zeros(half), sn, zeros(HEAD_DIM - 2 * half)], axis=0).T
        for idx, tab in enumerate((c, a, b)):
            tab_ref[0, idx] = tab * q_scale
            tab_ref[1, idx] = tab

    def step():
        part = j // tiles_per_part
        for c0 in range(0, wr_ref.shape[1], MXU_COLS):
            y = _dot(xn_ref[...], wr_ref[:, c0:c0 + MXU_COLS])
            for h0 in range(0, MXU_COLS, HEAD_DIM):
                yh = y[:, h0:h0 + HEAD_DIM]
                up = pltpu.roll(yh, HEAD_DIM - ROPE_DIM // 2, 1)
                dn = pltpu.roll(yh, ROPE_DIM // 2, 1)
                qk_ref[:, c0 + h0:c0 + h0 + HEAD_DIM] = (
                    yh * tab_ref[part, 0] + up * tab_ref[part, 1] + dn * tab_ref[part, 2])
        v_ref[...] = _dot(xn_ref[...], wv_ref[...])

    @pl.when(j == 0)
    def _():
        prologue()
        step()

    @pl.when(j > 0)
    def _():
        step()


def _rope_inv_freq():
    freqs = ROPE_THETA ** (-np.arange(0, ROPE_DIM, 2, dtype=np.float64) / ROPE_DIM)
    return jnp.asarray(freqs.astype(np.float32).reshape(ROPE_DIM // 2, 1))


def _norm_matmul(x, g, w, cast=(), *, tm=1024, tn=1024):
    s, d = x.shape
    n = w.shape[1]
    tm, tn = _pick(s, tm), _pick(n, tn)
    cast_specs = [_cast_specs(cw, lead, s // tm, n // tn) for cw, lead in cast]
    in_specs = [
        pl.BlockSpec((tm, d), lambda i, j: (i, 0)),
        pl.BlockSpec((1, d), lambda i, j: (0, 0)),
        pl.BlockSpec((d, tn), lambda i, j: (0, j)),
    ]
    outs = pl.pallas_call(
        functools.partial(_norm_matmul_kernel, n_cast=len(cast)),
        grid=(s // tm, n // tn),
        in_specs=in_specs + [src for src, _ in cast_specs],
        out_specs=[pl.BlockSpec((tm, tn), lambda i, j: (i, j))] + [dst for _, dst in cast_specs],
        out_shape=[jax.ShapeDtypeStruct((s, n), BF16)]
        + [jax.ShapeDtypeStruct(cw.shape[-2:], BF16) for cw, _ in cast],
        scratch_shapes=[pltpu.VMEM((tm, d), BF16)],
        compiler_params=_params(("parallel", "arbitrary")),
        name="norm_matmul",
    )(x, g.reshape(1, d), w, *[cw for cw, _ in cast])
    return outs[0] if not cast else outs


def _qkv_proj(x, g, w, positions, *, tm=1024, rope_tile=ROPE_TILE):
    s, d = x.shape
    assert w.shape == (d, 3 * d)
    tm, rope_tile = _pick(s, tm), _pick(d, rope_tile)
    n_j = 2 * d // rope_tile
    v_tile = d // n_j
    assert rope_tile % MXU_COLS == 0 and MXU_COLS % HEAD_DIM == 0 and v_tile % HEAD_DIM == 0
    return pl.pallas_call(
        functools.partial(_qkv_proj_kernel, tiles_per_part=d // rope_tile,
                          q_scale=LOG2E / np.sqrt(HEAD_DIM)),
        grid=(s // tm, n_j),
        in_specs=[
            pl.BlockSpec((tm, d), lambda i, j: (i, 0)),
            pl.BlockSpec((1, d), lambda i, j: (0, 0)),
            pl.BlockSpec((d, rope_tile), lambda i, j: (0, j)),
            pl.BlockSpec((d, v_tile), lambda i, j: (0, 2 * d // v_tile + j)),
            pl.BlockSpec((None, 1, tm), lambda i, j: (i, 0, 0)),
            pl.BlockSpec((ROPE_DIM // 2, 1), lambda i, j: (0, 0)),
        ],
        out_specs=[pl.BlockSpec((tm, rope_tile), lambda i, j: (i, j)),
                   pl.BlockSpec((tm, v_tile), lambda i, j: (i, j))],
        out_shape=[jax.ShapeDtypeStruct((s, 2 * d), F32), jax.ShapeDtypeStruct((s, d), F32)],
        scratch_shapes=[pltpu.VMEM((tm, d), BF16), pltpu.VMEM((2, 3, tm, HEAD_DIM), F32)],
        compiler_params=_params(("parallel", "arbitrary")),
        name="qkv_proj",
    )(x, g.reshape(1, d), w, w, positions.reshape(s // tm, 1, tm), _rope_inv_freq())


def _proj_residual_kernel(a_ref, h_ref, w_ref, o_ref):
    o_ref[...] = h_ref[...] + _dot(a_ref[...], w_ref[...])


def _proj_residual(a, h, w, *, tm=512):
    s, d = h.shape
    tm = _pick(s, tm)
    row = lambda width: pl.BlockSpec((tm, width), lambda i: (i, 0))
    return pl.pallas_call(
        _proj_residual_kernel,
        grid=(s // tm,),
        in_specs=[row(a.shape[1]), row(d), pl.BlockSpec(w.shape, lambda i: (0, 0))],
        out_specs=row(d),
        out_shape=jax.ShapeDtypeStruct((s, d), F32),
        compiler_params=_params(("parallel",)),
        name="proj_residual",
    )(a, h, w)


def _mixer_kernel(zc_ref, zh_ref, h_ref, cw_ref, pw_ref, ps_ref, wo_ref, o_ref, y_ref, *, tm, cdim, pgroup):
    i = pl.program_id(0)
    has_prev = i > 0
    chunk = pgroup

    def ext(col0):
        sl = slice(col0, col0 + chunk)
        halo = jnp.where(has_prev, zh_ref[:, sl].astype(F32), 0.0)
        return jnp.concatenate([halo, zc_ref[:, sl].astype(F32)], axis=0)

    for c0 in range(0, cdim, chunk):
        cx = ext(cdim + c0) * ext(2 * cdim + c0)
        w = cw_ref[:, c0:c0 + chunk]
        conv = (w[2:3] * cx + w[1:2] * pltpu.roll(cx, 1, 0) + w[0:1] * pltpu.roll(cx, 2, 0))[HALO:]
        gate_b = zc_ref[:, c0:c0 + chunk].astype(F32)
        y_ref[:, c0:c0 + chunk] = (gate_b * conv).astype(BF16)

    t = i * tm + lax.broadcasted_iota(jnp.int32, (tm, 1), 0)
    for gi, win in enumerate(POOL_WINDOWS):
        col0 = 3 * cdim + gi * chunk
        acc = ext(col0)
        sh = 1
        while sh < win:
            acc = acc + pltpu.roll(acc, sh, 0)
            sh *= 2
        inv_cnt = 1.0 / jnp.minimum(t + 1, win).astype(F32)
        u = zc_ref[:, col0:col0 + chunk].astype(F32)
        pooled = acc[HALO:] * inv_cnt - u
        mixed = _dot(pooled.astype(BF16), pw_ref[gi])
        y_ref[:, cdim + gi * chunk: cdim + (gi + 1) * chunk] = (
            mixed * ps_ref[:, gi * chunk:(gi + 1) * chunk]).astype(BF16)

    o_ref[...] = h_ref[...] + _dot(y_ref[...], wo_ref[...])


def _mixer(z, h, conv_w, pool_w, pool_scale, w_out, *, tm=512):
    s, d = h.shape
    cdim = d // 2
    pgroup = pool_w.shape[-1]
    tm = _pick(s, tm)
    assert tm % HALO == 0 and cdim % pgroup == 0 and max(POOL_WINDOWS) <= HALO
    hb = tm // HALO
    return pl.pallas_call(
        functools.partial(_mixer_kernel, tm=tm, cdim=cdim, pgroup=pgroup),
        grid=(s // tm,),
        in_specs=[
            pl.BlockSpec((tm, 4 * cdim), lambda i: (i, 0)),
            pl.BlockSpec((HALO, 4 * cdim), lambda i: (jnp.maximum(i * hb - 1, 0), 0)),
            pl.BlockSpec((tm, d), lambda i: (i, 0)),
            pl.BlockSpec((CONV_WIDTH, cdim), lambda i: (0, 0)),
            pl.BlockSpec(pool_w.shape, lambda i: (0, 0, 0)),
            pl.BlockSpec((1, cdim), lambda i: (0, 0)),
            pl.BlockSpec((d, d), lambda i: (0, 0)),
        ],
        out_specs=pl.BlockSpec((tm, d), lambda i: (i, 0)),
        out_shape=jax.ShapeDtypeStruct((s, d), F32),
        scratch_shapes=[pltpu.VMEM((tm, d), BF16)],
        compiler_params=_params(("parallel",)),
        name="mixer",
    )(z, z, h, conv_w, pool_w, pool_scale.reshape(1, cdim), w_out)


def _attn_kernel(q_ref, k_ref, v_ref, o_ref, *scr, dils, span):
    n_br = len(dils)
    ks, vs, qs, ogs, lgs = (scr[i * n_br:(i + 1) * n_br] for i in range(5))
    t = pl.program_id(1)
    blk = ATT_BLOCK
    tile = q_ref.shape[0]

    @pl.when(t == 0)
    def _():
        for kd, vd in zip(ks, vs):
            kd[:, :blk, :] = jnp.zeros((kd.shape[0], blk, HEAD_DIM), BF16)
            vd[:, :blk, :] = jnp.zeros((vd.shape[0], blk, HEAD_DIM), BF16)

    @pl.when(t > 0)
    def _():
        for kd, vd, d in zip(ks, vs, dils):
            n = tile // d
            kd[:, :blk, :] = kd[:, n:n + blk, :]
            vd[:, :blk, :] = vd[:, n:n + blk, :]

    for kd, vd, qd, d in zip(ks, vs, qs, dils):
        n = tile // d
        for r in range(d):
            rows = pl.ds(r, n, stride=d) if d > 1 else slice(None)
            kd[r, blk:, :] = k_ref[rows, :].astype(BF16)
            vd[r, blk:, :] = v_ref[rows, :].astype(BF16)
            qd[r] = q_ref[rows, :].astype(BF16)

    lm = lax.broadcasted_iota(jnp.int32, (8, blk), 1) - lax.broadcasted_iota(jnp.int32, (8, blk), 0)
    never = jnp.where(t == 0, 2 * blk, 0)

    def masked(s, first):
        slabs = []
        for j8 in range(0, blk, 8):
            lo = j8 + blk - span
            prev = jnp.where(lm >= (lo + never if first else lo), s[j8:j8 + 8, :blk], NEG_INF)
            cur = jnp.where(lm <= j8, s[j8:j8 + 8, blk:], NEG_INF)
            slabs.append(jnp.concatenate([prev, cur], axis=1))
        return jnp.concatenate(slabs, axis=0)

    for g, d in enumerate(dils):
        n = tile // d
        for r in range(d):
            for b in range(n // blk):
                q = qs[g][r, b * blk:(b + 1) * blk, :]
                kk = ks[g][r, b * blk:(b + 2) * blk, :]
                vv = vs[g][r, b * blk:(b + 2) * blk, :]
                s = lax.dot_general(q, kk, (((1,), (1,)), ((), ())), preferred_element_type=F32)
                s = masked(s, b == 0)
                m = jnp.max(s, axis=-1, keepdims=True)
                p = jnp.exp2(s - m)
                l = jnp.sum(p, axis=-1, keepdims=True)
                o = _dot(p.astype(BF16), vv) / l
                lse = m * LN2 + jnp.log(l)
                rows = pl.ds(b * blk * d + r, blk, stride=d) if d > 1 else slice(b * blk, (b + 1) * blk)
                ogs[g][rows, :] = o
                lgs[g][rows, :] = jnp.broadcast_to(lse, (blk, HEAD_DIM))

    for c0 in range(0, tile, COMBINE_ROWS):
        rows = slice(c0, c0 + COMBINE_ROWS)
        ls = [lg[rows, :] for lg in lgs]
        m = functools.reduce(jnp.maximum, ls)
        es = [jnp.exp(x - m) for x in ls]
        tot = functools.reduce(lambda x, y: x + y, es)
        acc = functools.reduce(lambda x, y: x + y, [e * og[rows, :] for e, og in zip(es, ogs)])
        o_ref[rows, :] = (acc / tot).astype(o_ref.dtype)


def _attention(qk, v, *, tile=ATT_TILE):
    s, d_model = v.shape
    n_heads = d_model // HEAD_DIM
    dils = tuple(d for _, d in DIL_BRANCHES)
    spans = {w // d for w, d in DIL_BRANCHES}
    assert len(spans) == 1 and s % tile == 0 and all(tile % (ATT_BLOCK * d) == 0 for d in dils)
    span = spans.pop()
    assert ATT_BLOCK - 1 <= span <= ATT_BLOCK
    kv = [pltpu.VMEM((d, ATT_BLOCK + tile // d, HEAD_DIM), BF16) for d in dils]
    qd = [pltpu.VMEM((d, tile // d, HEAD_DIM), BF16) for d in dils]
    acc = [pltpu.VMEM((tile, HEAD_DIM), F32) for _ in dils]
    blockspec = lambda part: pl.BlockSpec((tile, HEAD_DIM), lambda h, t: (t, part * n_heads + h))
    return pl.pallas_call(
        functools.partial(_attn_kernel, dils=dils, span=span),
        grid=(n_heads, s // tile),
        in_specs=[blockspec(0), blockspec(1), blockspec(0)],
        out_specs=pl.BlockSpec((tile, HEAD_DIM), lambda h, t: (t, h)),
        out_shape=jax.ShapeDtypeStruct((s, d_model), BF16),
        scratch_shapes=kv + kv + qd + acc + acc,
        compiler_params=_params(("parallel", "arbitrary")),
        name="attn",
    )(qk, qk, v)


def kernel(x, positions, norm_g, ffn_w1, ffn_w3, ffn_w2, mix_w_in, conv_w, pool_w, pool_scale,
           mix_w_out, attn_w_qkv, attn_w_o, final_g):
    b, s, d = x.shape
    depth = norm_g.shape[0]
    bf = lambda w: w.astype(BF16)
    ffns = [(i, j) for i in range(depth) for j in range(2)]
    wts = {ffns[0]: [bf(w[ffns[0]]) for w in (ffn_w1, ffn_w3, ffn_w2)]}

    def ffn(h, key, g, final_g=None):
        nxt = ffns.index(key) + 1
        if nxt < len(ffns) and ffns[nxt] not in wts:
            h, *wts[ffns[nxt]] = _ffn(h, g, *wts[key], final_g=final_g,
                                      cast=[(w, ffns[nxt]) for w in (ffn_w1, ffn_w3, ffn_w2)])
            return h
        return _ffn(h, g, *wts[key], final_g=final_g)

    proj_w = {}
    outs = []
    for bi in range(b):
        h = x[bi]
        for i in range(depth):
            h = ffn(h, (i, 0), norm_g[i, 0])
            j = i // 2
            if i % 2 == 0:
                later = {"mix_out": (mix_w_out, (j,))}
                if i + 1 < depth:
                    later.update(qkv=(attn_w_qkv, (j,)), attn_out=(attn_w_o, (j,)))
                later = {k: v for k, v in later.items() if (k, v[1]) not in proj_w}
                res = _norm_matmul(h, norm_g[i, 1], bf(mix_w_in[j]), cast=list(later.values()))
                z, cast_w = (res[0], res[1:]) if later else (res, [])
                proj_w.update({(k, v[1]): cw for (k, v), cw in zip(later.items(), cast_w)})
                h = _mixer(z, h, conv_w[j], bf(pool_w[j]), pool_scale[j], proj_w["mix_out", (j,)])
            else:
                w_qkv = proj_w.get(("qkv", (j,)))
                w_o = proj_w.get(("attn_out", (j,)))
                w_qkv = bf(attn_w_qkv[j]) if w_qkv is None else w_qkv
                w_o = bf(attn_w_o[j]) if w_o is None else w_o
                qk, v = _qkv_proj(h, norm_g[i, 1], w_qkv, positions[bi])
                h = _proj_residual(_attention(qk, v), h, w_o)
            h = ffn(h, (i, 1), norm_g[i, 2], final_g=final_g if i == depth - 1 else None)
        outs.append(h.reshape(1, s, d))
    return outs[0] if b == 1 else jnp.concatenate(outs, axis=0)
```

```python
import functools

import numpy as np
import jax
import jax.numpy as jnp
from jax import lax
from jax.experimental import pallas as pl
from jax.experimental.pallas import tpu as pltpu

F32 = jnp.float32
BF16 = jnp.bfloat16

HEAD_DIM = 128
ROPE_DIM = HEAD_DIM // 4
ROPE_THETA = 500000.0
DIL_BRANCHES = ((128, 1), (512, 4), (2048, 16))
ATT_BLOCK = 128
ATT_TILE = ATT_BLOCK * max(d for _, d in DIL_BRANCHES)
COMBINE_ROWS = 256
EPILOGUE_ROWS = 256
CONV_WIDTH = 3
POOL_WINDOWS = (2, 4, 8, 16)
HALO = 16
RMS_EPS = 1e-6
NEG_INF = -1e30
LOG2E = float(np.log2(np.e))
LN2 = float(np.log(2.0))

MXU_COLS = 256
FFN_TILE = 512
PROJ_TILE = 1024
ROPE_TILE = 512
V7X_VMEM_BYTES = 64 * 1024 * 1024
VMEM_LIMIT = V7X_VMEM_BYTES - 6 * 1024 * 1024


def _params(sem):
    return pltpu.CompilerParams(dimension_semantics=sem, vmem_limit_bytes=VMEM_LIMIT)


def _rms(x, g):
    ms = jnp.mean(x * x, axis=-1, keepdims=True)
    return x * lax.rsqrt(ms + RMS_EPS) * g


def _dot(a, b):
    return jnp.dot(a, b, preferred_element_type=F32)


def _pick(n, pref):
    t = min(n, pref)
    while n % t:
        t //= 2
    return t


def _ffn_kernel(x_ref, g_ref, w1_ref, w3_ref, w2_ref, *rest, n_f, final_norm, n_cast):
    rest = list(rest)
    fg_ref = rest.pop(0) if final_norm else None
    cast_in = [rest.pop(0) for _ in range(n_cast)]
    o_ref = rest.pop(0)
    cast_out = [rest.pop(0) for _ in range(n_cast)]
    (xn_ref,) = rest
    f = pl.program_id(1)

    def step():
        xn = xn_ref[...]
        h1 = _dot(xn, w1_ref[...])
        h3 = _dot(xn, w3_ref[...])
        a = (h1 * jax.nn.sigmoid(h1) * (0.5 * h3)).astype(BF16)
        o_ref[...] += _dot(a, w2_ref[...])
        for src, dst in zip(cast_in, cast_out):
            dst[...] = src[...].astype(BF16)

    @pl.when(f == 0)
    def _():
        xn_ref[...] = _rms(x_ref[...], g_ref[...]).astype(BF16)
        o_ref[...] = x_ref[...]
        step()

    @pl.when(f > 0)
    def _():
        step()

    if final_norm:
        @pl.when(f == n_f - 1)
        def _():
            def chunk(c, carry):
                rows = pl.ds(pl.multiple_of(c * EPILOGUE_ROWS, EPILOGUE_ROWS), EPILOGUE_ROWS)
                o_ref[rows, :] = _rms(o_ref[rows, :], fg_ref[...])
                return carry

            lax.fori_loop(0, o_ref.shape[0] // EPILOGUE_ROWS, chunk, 0)


def _cast_specs(w, lead, n_i, n_f, col_tiled=False):
    r, c = w.shape[-2:]
    wide = c >= r
    shape = (r // n_i, c // n_f) if wide else (r // n_f, c // n_i)
    assert shape[0] * (n_i if wide else n_f) == r and shape[1] * (n_f if wide else n_i) == c
    assert shape[0] % 16 == 0 and shape[1] % 128 == 0, shape
    block = (lambda i, f: (i, f)) if wide else (lambda i, f: (f, i))
    src = pl.BlockSpec((None,) * len(lead) + shape, lambda i, f: tuple(lead) + block(i, f))
    if col_tiled:
        assert wide
        return src, pl.BlockSpec((None,) + shape, lambda i, f: (f, i, 0)), (n_f, r, shape[1])
    return src, pl.BlockSpec(shape, block), (r, c)


def _col_tiled(w, tf):
    d, ff = w.shape
    return w.reshape(d, ff // tf, tf).transpose(1, 0, 2)


def _ffn(x, g, w1, w3, w2, final_g=None, cast=(), *, tm=1024):
    s, d = x.shape
    n_f, _, tf = w1.shape
    tm = _pick(s, tm)
    n_i = s // tm
    final_norm = final_g is not None
    in_specs = [
        pl.BlockSpec((tm, d), lambda i, f: (i, 0)),
        pl.BlockSpec((1, d), lambda i, f: (0, 0)),
        pl.BlockSpec((None, d, tf), lambda i, f: (f, 0, 0)),
        pl.BlockSpec((None, d, tf), lambda i, f: (f, 0, 0)),
        pl.BlockSpec((tf, d), lambda i, f: (f, 0)),
    ]
    args = [x, g.reshape(1, d), w1, w3, w2]
    if final_norm:
        in_specs.append(pl.BlockSpec((1, d), lambda i, f: (0, 0)))
        args.append(final_g.reshape(1, d))
    cast_specs = [_cast_specs(w, lead, n_i, n_f, tiled) for w, lead, tiled in cast]
    outs = pl.pallas_call(
        functools.partial(_ffn_kernel, n_f=n_f, final_norm=final_norm, n_cast=len(cast)),
        grid=(n_i, n_f),
        in_specs=in_specs + [src for src, _, _ in cast_specs],
        out_specs=[pl.BlockSpec((tm, d), lambda i, f: (i, 0))] + [dst for _, dst, _ in cast_specs],
        out_shape=[jax.ShapeDtypeStruct((s, d), F32)]
        + [jax.ShapeDtypeStruct(shape, BF16) for _, _, shape in cast_specs],
        scratch_shapes=[pltpu.VMEM((tm, d), BF16)],
        compiler_params=_params(("parallel", "arbitrary")),
        name="ffn",
    )(*args, *[w for w, _, _ in cast])
    return outs[0] if not cast else outs


def _norm_matmul_kernel(x_ref, g_ref, w_ref, *rest, n_cast):
    cast_in, (o_ref, *cast_out), xn_ref = rest[:n_cast], rest[n_cast:2 * n_cast + 1], rest[-1]
    j = pl.program_id(1)

    def step():
        for src, dst in zip(cast_in, cast_out):
            dst[...] = src[...].astype(BF16)
        o_ref[...] = _dot(xn_ref[...], w_ref[...]).astype(o_ref.dtype)

    @pl.when(j == 0)
    def _():
        xn_ref[...] = _rms(x_ref[...], g_ref[...]).astype(BF16)
        step()

    @pl.when(j > 0)
    def _():
        step()


def _qkv_proj_kernel(x_ref, g_ref, wr_ref, wv_ref, pos_ref, inv_ref, qk_ref, v_ref, xn_ref, tab_ref,
                     *, tiles_per_part, q_scale):
    j = pl.program_id(1)

    def prologue():
        xn_ref[...] = _rms(x_ref[...], g_ref[...]).astype(BF16)
        half = ROPE_DIM // 2
        ang = inv_ref[...] * pos_ref[...].astype(F32)
        cs, sn = jnp.cos(ang), jnp.sin(ang)
        zeros = lambda n: jnp.zeros((n, ang.shape[1]), F32)
        c = jnp.concatenate([cs, cs, jnp.ones((HEAD_DIM - 2 * half, ang.shape[1]), F32)], axis=0).T
        a = jnp.concatenate([-sn, zeros(HEAD_DIM - half)], axis=0).T
        b = jnp.concatenate([zeros(half), sn, zeros(HEAD_DIM - 2 * half)], axis=0).T
        for idx, tab in enumerate((c, a, b)):
            tab_ref[0, idx] = tab * q_scale
            tab_ref[1, idx] = tab

    def step():
        part = j // tiles_per_part
        for c0 in range(0, wr_ref.shape[1], MXU_COLS):
            y = _dot(xn_ref[...], wr_ref[:, c0:c0 + MXU_COLS])
            for h0 in range(0, MXU_COLS, HEAD_DIM):
                yh = y[:, h0:h0 + HEAD_DIM]
                up = pltpu.roll(yh, HEAD_DIM - ROPE_DIM // 2, 1)
                dn = pltpu.roll(yh, ROPE_DIM // 2, 1)
                qk_ref[:, c0 + h0:c0 + h0 + HEAD_DIM] = (
                    yh * tab_ref[part, 0] + up * tab_ref[part, 1] + dn * tab_ref[part, 2])
        v_ref[...] = _dot(xn_ref[...], wv_ref[...])

    @pl.when(j == 0)
    def _():
        prologue()
        step()

    @pl.when(j > 0)
    def _():
        step()


def _rope_inv_freq():
    freqs = ROPE_THETA ** (-np.arange(0, ROPE_DIM, 2, dtype=np.float64) / ROPE_DIM)
    return jnp.asarray(freqs.astype(np.float32).reshape(ROPE_DIM // 2, 1))


def _norm_matmul(x, g, w, cast=(), *, tm=1024):
    s, d = x.shape
    n_j, _, tn = w.shape
    n = n_j * tn
    tm = _pick(s, tm)
    cast_specs = [_cast_specs(cw, lead, s // tm, n_j) for cw, lead in cast]
    in_specs = [
        pl.BlockSpec((tm, d), lambda i, j: (i, 0)),
        pl.BlockSpec((1, d), lambda i, j: (0, 0)),
        pl.BlockSpec((None, d, tn), lambda i, j: (j, 0, 0)),
    ]
    outs = pl.pallas_call(
        functools.partial(_norm_matmul_kernel, n_cast=len(cast)),
        grid=(s // tm, n // tn),
        in_specs=in_specs + [src for src, _, _ in cast_specs],
        out_specs=[pl.BlockSpec((tm, tn), lambda i, j: (i, j))] + [dst for _, dst, _ in cast_specs],
        out_shape=[jax.ShapeDtypeStruct((s, n), BF16)]
        + [jax.ShapeDtypeStruct(shape, BF16) for _, _, shape in cast_specs],
        scratch_shapes=[pltpu.VMEM((tm, d), BF16)],
        compiler_params=_params(("parallel", "arbitrary")),
        name="norm_matmul",
    )(x, g.reshape(1, d), w, *[cw for cw, _ in cast])
    return outs[0] if not cast else outs


def _qkv_proj(x, g, w, positions, *, tm=1024, rope_tile=ROPE_TILE):
    s, d = x.shape
    assert w.shape == (d, 3 * d)
    tm, rope_tile = _pick(s, tm), _pick(d, rope_tile)
    n_j = 2 * d // rope_tile
    v_tile = d // n_j
    assert rope_tile % MXU_COLS == 0 and MXU_COLS % HEAD_DIM == 0 and v_tile % HEAD_DIM == 0
    return pl.pallas_call(
        functools.partial(_qkv_proj_kernel, tiles_per_part=d // rope_tile,
                          q_scale=LOG2E / np.sqrt(HEAD_DIM)),
        grid=(s // tm, n_j),
        in_specs=[
            pl.BlockSpec((tm, d), lambda i, j: (i, 0)),
            pl.BlockSpec((1, d), lambda i, j: (0, 0)),
            pl.BlockSpec((d, rope_tile), lambda i, j: (0, j)),
            pl.BlockSpec((d, v_tile), lambda i, j: (0, 2 * d // v_tile + j)),
            pl.BlockSpec((None, 1, tm), lambda i, j: (i, 0, 0)),
            pl.BlockSpec((ROPE_DIM // 2, 1), lambda i, j: (0, 0)),
        ],
        out_specs=[pl.BlockSpec((tm, rope_tile), lambda i, j: (i, j)),
                   pl.BlockSpec((tm, v_tile), lambda i, j: (i, j))],
        out_shape=[jax.ShapeDtypeStruct((s, 2 * d), F32), jax.ShapeDtypeStruct((s, d), F32)],
        scratch_shapes=[pltpu.VMEM((tm, d), BF16), pltpu.VMEM((2, 3, tm, HEAD_DIM), F32)],
        compiler_params=_params(("parallel", "arbitrary")),
        name="qkv_proj",
    )(x, g.reshape(1, d), w, w, positions.reshape(s // tm, 1, tm), _rope_inv_freq())


def _proj_residual_kernel(a_ref, h_ref, w_ref, o_ref):
    o_ref[...] = h_ref[...] + _dot(a_ref[...], w_ref[...])


def _proj_residual(a, h, w, *, tm=512):
    s, d = h.shape
    tm = _pick(s, tm)
    row = lambda width: pl.BlockSpec((tm, width), lambda i: (i, 0))
    return pl.pallas_call(
        _proj_residual_kernel,
        grid=(s // tm,),
        in_specs=[row(a.shape[1]), row(d), pl.BlockSpec(w.shape, lambda i: (0, 0))],
        out_specs=row(d),
        out_shape=jax.ShapeDtypeStruct((s, d), F32),
        compiler_params=_params(("parallel",)),
        name="proj_residual",
    )(a, h, w)


def _mixer_kernel(zc_ref, zh_ref, h_ref, cw_ref, pw_ref, ps_ref, wo_ref, o_ref, y_ref, *, tm, cdim, pgroup):
    i = pl.program_id(0)
    has_prev = i > 0
    chunk = pgroup

    def ext(col0):
        sl = slice(col0, col0 + chunk)
        halo = jnp.where(has_prev, zh_ref[:, sl].astype(F32), 0.0)
        return jnp.concatenate([halo, zc_ref[:, sl].astype(F32)], axis=0)

    for c0 in range(0, cdim, chunk):
        cx = ext(cdim + c0) * ext(2 * cdim + c0)
        w = cw_ref[:, c0:c0 + chunk]
        conv = (w[2:3] * cx + w[1:2] * pltpu.roll(cx, 1, 0) + w[0:1] * pltpu.roll(cx, 2, 0))[HALO:]
        gate_b = zc_ref[:, c0:c0 + chunk].astype(F32)
        y_ref[:, c0:c0 + chunk] = (gate_b * conv).astype(BF16)

    t = i * tm + lax.broadcasted_iota(jnp.int32, (tm, 1), 0)
    for gi, win in enumerate(POOL_WINDOWS):
        col0 = 3 * cdim + gi * chunk
        acc = ext(col0)
        sh = 1
        while sh < win:
            acc = acc + pltpu.roll(acc, sh, 0)
            sh *= 2
        inv_cnt = 1.0 / jnp.minimum(t + 1, win).astype(F32)
        u = zc_ref[:, col0:col0 + chunk].astype(F32)
        pooled = acc[HALO:] * inv_cnt - u
        mixed = _dot(pooled.astype(BF16), pw_ref[gi])
        y_ref[:, cdim + gi * chunk: cdim + (gi + 1) * chunk] = (
            mixed * ps_ref[:, gi * chunk:(gi + 1) * chunk]).astype(BF16)

    o_ref[...] = h_ref[...] + _dot(y_ref[...], wo_ref[...])


def _mixer(z, h, conv_w, pool_w, pool_scale, w_out, *, tm=512):
    s, d = h.shape
    cdim = d // 2
    pgroup = pool_w.shape[-1]
    tm = _pick(s, tm)
    assert tm % HALO == 0 and cdim % pgroup == 0 and max(POOL_WINDOWS) <= HALO
    hb = tm // HALO
    return pl.pallas_call(
        functools.partial(_mixer_kernel, tm=tm, cdim=cdim, pgroup=pgroup),
        grid=(s // tm,),
        in_specs=[
            pl.BlockSpec((tm, 4 * cdim), lambda i: (i, 0)),
            pl.BlockSpec((HALO, 4 * cdim), lambda i: (jnp.maximum(i * hb - 1, 0), 0)),
            pl.BlockSpec((tm, d), lambda i: (i, 0)),
            pl.BlockSpec((CONV_WIDTH, cdim), lambda i: (0, 0)),
            pl.BlockSpec(pool_w.shape, lambda i: (0, 0, 0)),
            pl.BlockSpec((1, cdim), lambda i: (0, 0)),
            pl.BlockSpec((d, d), lambda i: (0, 0)),
        ],
        out_specs=pl.BlockSpec((tm, d), lambda i: (i, 0)),
        out_shape=jax.ShapeDtypeStruct((s, d), F32),
        scratch_shapes=[pltpu.VMEM((tm, d), BF16)],
        compiler_params=_params(("parallel",)),
        name="mixer",
    )(z, z, h, conv_w, pool_w, pool_scale.reshape(1, cdim), w_out)


def _attn_kernel(q_ref, k_ref, v_ref, o_ref, *scr, dils, span):
    n_br = len(dils)
    ks, vs, qs, ogs, lgs = (scr[i * n_br:(i + 1) * n_br] for i in range(5))
    t = pl.program_id(1)
    blk = ATT_BLOCK
    tile = q_ref.shape[0]

    @pl.when(t == 0)
    def _():
        for kd, vd in zip(ks, vs):
            kd[:, :blk, :] = jnp.zeros((kd.shape[0], blk, HEAD_DIM), BF16)
            vd[:, :blk, :] = jnp.zeros((vd.shape[0], blk, HEAD_DIM), BF16)

    @pl.when(t > 0)
    def _():
        for kd, vd, d in zip(ks, vs, dils):
            n = tile // d
            kd[:, :blk, :] = kd[:, n:n + blk, :]
            vd[:, :blk, :] = vd[:, n:n + blk, :]

    for kd, vd, qd, d in zip(ks, vs, qs, dils):
        n = tile // d
        for r in range(d):
            rows = pl.ds(r, n, stride=d) if d > 1 else slice(None)
            kd[r, blk:, :] = k_ref[rows, :].astype(BF16)
            vd[r, blk:, :] = v_ref[rows, :].astype(BF16)
            qd[r] = q_ref[rows, :].astype(BF16)

    lm = lax.broadcasted_iota(jnp.int32, (8, blk), 1) - lax.broadcasted_iota(jnp.int32, (8, blk), 0)
    never = jnp.where(t == 0, 2 * blk, 0)

    def masked(s, first):
        slabs = []
        for j8 in range(0, blk, 8):
            lo = j8 + blk - span
            prev = jnp.where(lm >= (lo + never if first else lo), s[j8:j8 + 8, :blk], NEG_INF)
            cur = jnp.where(lm <= j8, s[j8:j8 + 8, blk:], NEG_INF)
            slabs.append(jnp.concatenate([prev, cur], axis=1))
        return jnp.concatenate(slabs, axis=0)

    for g, d in enumerate(dils):
        n = tile // d
        for r in range(d):
            for b in range(n // blk):
                q = qs[g][r, b * blk:(b + 1) * blk, :]
                kk = ks[g][r, b * blk:(b + 2) * blk, :]
                vv = vs[g][r, b * blk:(b + 2) * blk, :]
                s = lax.dot_general(q, kk, (((1,), (1,)), ((), ())), preferred_element_type=F32)
                s = masked(s, b == 0)
                m = jnp.max(s, axis=-1, keepdims=True)
                p = jnp.exp2(s - m)
                l = jnp.sum(p, axis=-1, keepdims=True)
                o = _dot(p.astype(BF16), vv) / l
                lse = m * LN2 + jnp.log(l)
                rows = pl.ds(b * blk * d + r, blk, stride=d) if d > 1 else slice(b * blk, (b + 1) * blk)
                ogs[g][rows, :] = o
                lgs[g][rows, :] = jnp.broadcast_to(lse, (blk, HEAD_DIM))

    for c0 in range(0, tile, COMBINE_ROWS):
        rows = slice(c0, c0 + COMBINE_ROWS)
        ls = [lg[rows, :] for lg in lgs]
        m = functools.reduce(jnp.maximum, ls)
        es = [jnp.exp(x - m) for x in ls]
        tot = functools.reduce(lambda x, y: x + y, es)
        acc = functools.reduce(lambda x, y: x + y, [e * og[rows, :] for e, og in zip(es, ogs)])
        o_ref[rows, :] = (acc / tot).astype(o_ref.dtype)


def _attention(qk, v, *, tile=ATT_TILE):
    s, d_model = v.shape
    n_heads = d_model // HEAD_DIM
    dils = tuple(d for _, d in DIL_BRANCHES)
    spans = {w // d for w, d in DIL_BRANCHES}
    assert len(spans) == 1 and s % tile == 0 and all(tile % (ATT_BLOCK * d) == 0 for d in dils)
    span = spans.pop()
    assert ATT_BLOCK - 1 <= span <= ATT_BLOCK
    kv = [pltpu.VMEM((d, ATT_BLOCK + tile // d, HEAD_DIM), BF16) for d in dils]
    qd = [pltpu.VMEM((d, tile // d, HEAD_DIM), BF16) for d in dils]
    acc = [pltpu.VMEM((tile, HEAD_DIM), F32) for _ in dils]
    blockspec = lambda part: pl.BlockSpec((tile, HEAD_DIM), lambda h, t: (t, part * n_heads + h))
    return pl.pallas_call(
        functools.partial(_attn_kernel, dils=dils, span=span),
        grid=(n_heads, s // tile),
        in_specs=[blockspec(0), blockspec(1), blockspec(0)],
        out_specs=pl.BlockSpec((tile, HEAD_DIM), lambda h, t: (t, h)),
        out_shape=jax.ShapeDtypeStruct((s, d_model), BF16),
        scratch_shapes=kv + kv + qd + acc + acc,
        compiler_params=_params(("parallel", "arbitrary")),
        name="attn",
    )(qk, qk, v)


def kernel(x, positions, norm_g, ffn_w1, ffn_w3, ffn_w2, mix_w_in, conv_w, pool_w, pool_scale,
           mix_w_out, attn_w_qkv, attn_w_o, final_g):
    b, s, d = x.shape
    depth = norm_g.shape[0]
    bf = lambda w: w.astype(BF16)
    ffns = [(i, j) for i in range(depth) for j in range(2)]
    tf = _pick(ffn_w1.shape[-1], FFN_TILE)
    wts = {ffns[0]: [bf(_col_tiled(ffn_w1[ffns[0]], tf)), bf(_col_tiled(ffn_w3[ffns[0]], tf)),
                     bf(ffn_w2[ffns[0]])]}

    def ffn(h, key, g, final_g=None):
        nxt = ffns.index(key) + 1
        if nxt < len(ffns) and ffns[nxt] not in wts:
            cast = [(ffn_w1, ffns[nxt], True), (ffn_w3, ffns[nxt], True), (ffn_w2, ffns[nxt], False)]
            h, *wts[ffns[nxt]] = _ffn(h, g, *wts[key], final_g=final_g, cast=cast)
            return h
        return _ffn(h, g, *wts[key], final_g=final_g)

    proj_w = {}
    outs = []
    for bi in range(b):
        h = x[bi]
        for i in range(depth):
            h = ffn(h, (i, 0), norm_g[i, 0])
            j = i // 2
            if i % 2 == 0:
                later = {"mix_out": (mix_w_out, (j,))}
                if i + 1 < depth:
                    later.update(qkv=(attn_w_qkv, (j,)), attn_out=(attn_w_o, (j,)))
                later = {k: v for k, v in later.items() if (k, v[1]) not in proj_w}
                w_in = bf(_col_tiled(mix_w_in[j], _pick(mix_w_in.shape[-1], PROJ_TILE)))
                res = _norm_matmul(h, norm_g[i, 1], w_in, cast=list(later.values()))
                z, cast_w = (res[0], res[1:]) if later else (res, [])
                proj_w.update({(k, v[1]): cw for (k, v), cw in zip(later.items(), cast_w)})
                h = _mixer(z, h, conv_w[j], bf(pool_w[j]), pool_scale[j], proj_w["mix_out", (j,)])
            else:
                w_qkv = proj_w.get(("qkv", (j,)))
                w_o = proj_w.get(("attn_out", (j,)))
                w_qkv = bf(attn_w_qkv[j]) if w_qkv is None else w_qkv
                w_o = bf(attn_w_o[j]) if w_o is None else w_o
                qk, v = _qkv_proj(h, norm_g[i, 1], w_qkv, positions[bi])
                h = _proj_residual(_attention(qk, v), h, w_o)
            h = ffn(h, (i, 1), norm_g[i, 2], final_g=final_g if i == depth - 1 else None)
        outs.append(h.reshape(1, s, d))
    return outs[0] if b == 1 else jnp.concatenate(outs, axis=0)
```

```python
import functools

import numpy as np
import jax
import jax.numpy as jnp
from jax import lax
from jax.experimental import pallas as pl
from jax.experimental.pallas import tpu as pltpu

F32 = jnp.float32
BF16 = jnp.bfloat16

HEAD_DIM = 128
ROPE_DIM = HEAD_DIM // 4
ROPE_THETA = 500000.0
DIL_BRANCHES = ((128, 1), (512, 4), (2048, 16))
ATT_BLOCK = 128
ATT_TILE = ATT_BLOCK * max(d for _, d in DIL_BRANCHES)
COMBINE_ROWS = 256
EPILOGUE_ROWS = 256
CONV_WIDTH = 3
POOL_WINDOWS = (2, 4, 8, 16)
HALO = 16
RMS_EPS = 1e-6
NEG_INF = -1e30
LOG2E = float(np.log2(np.e))
LN2 = float(np.log(2.0))

MXU_COLS = 256
ROPE_TILE = 1024
PROJ_TILE = 2048
V7X_VMEM_BYTES = 64 * 1024 * 1024
VMEM_LIMIT = V7X_VMEM_BYTES - 6 * 1024 * 1024


def _params(sem):
    return pltpu.CompilerParams(dimension_semantics=sem, vmem_limit_bytes=VMEM_LIMIT)


def _rms(x, g):
    ms = jnp.mean(x * x, axis=-1, keepdims=True)
    return x * lax.rsqrt(ms + RMS_EPS) * g


def _dot(a, b):
    return jnp.dot(a, b, preferred_element_type=F32)


def _pick(n, pref):
    t = min(n, pref)
    while n % t:
        t //= 2
    return t


def _ffn_kernel(x_ref, g_ref, w1_ref, w3_ref, w2_ref, *rest, n_f, final_norm, n_cast):
    rest = list(rest)
    fg_ref = rest.pop(0) if final_norm else None
    cast_in = [rest.pop(0) for _ in range(n_cast)]
    o_ref = rest.pop(0)
    cast_out = [rest.pop(0) for _ in range(n_cast)]
    (xn_ref,) = rest
    f = pl.program_id(1)

    def step():
        xn = xn_ref[...]
        h1 = _dot(xn, w1_ref[...])
        h3 = _dot(xn, w3_ref[...])
        a = (h1 * jax.nn.sigmoid(h1) * (0.5 * h3)).astype(BF16)
        o_ref[...] += _dot(a, w2_ref[...])
        for src, dst in zip(cast_in, cast_out):
            dst[...] = src[...].astype(BF16)

    @pl.when(f == 0)
    def _():
        xn_ref[...] = _rms(x_ref[...], g_ref[...]).astype(BF16)
        o_ref[...] = x_ref[...]
        step()

    @pl.when(f > 0)
    def _():
        step()

    if final_norm:
        @pl.when(f == n_f - 1)
        def _():
            def chunk(c, carry):
                rows = pl.ds(pl.multiple_of(c * EPILOGUE_ROWS, EPILOGUE_ROWS), EPILOGUE_ROWS)
                o_ref[rows, :] = _rms(o_ref[rows, :], fg_ref[...])
                return carry

            lax.fori_loop(0, o_ref.shape[0] // EPILOGUE_ROWS, chunk, 0)


def _cast_specs(w, lead, n_i, n_f):
    r, c = w.shape[-2:]
    wide = c >= r
    shape = (r // n_i, c // n_f) if wide else (r // n_f, c // n_i)
    assert shape[0] * (n_i if wide else n_f) == r and shape[1] * (n_f if wide else n_i) == c
    assert shape[0] % 16 == 0 and shape[1] % 128 == 0, shape
    block = (lambda i, f: (i, f)) if wide else (lambda i, f: (f, i))
    src = pl.BlockSpec((None,) * len(lead) + shape, lambda i, f: tuple(lead) + block(i, f))
    return src, pl.BlockSpec(shape, block)


def _ffn(x, g, w1, w3, w2, final_g=None, cast=(), *, tm=1024, tf=512):
    s, d = x.shape
    ff = w1.shape[-1]
    tm, tf = _pick(s, tm), _pick(ff, tf)
    n_i, n_f = s // tm, ff // tf
    final_norm = final_g is not None
    in_specs = [
        pl.BlockSpec((tm, d), lambda i, f: (i, 0)),
        pl.BlockSpec((1, d), lambda i, f: (0, 0)),
        pl.BlockSpec((d, tf), lambda i, f: (0, f)),
        pl.BlockSpec((d, tf), lambda i, f: (0, f)),
        pl.BlockSpec((tf, d), lambda i, f: (f, 0)),
    ]
    args = [x, g.reshape(1, d), w1, w3, w2]
    if final_norm:
        in_specs.append(pl.BlockSpec((1, d), lambda i, f: (0, 0)))
        args.append(final_g.reshape(1, d))
    cast_specs = [_cast_specs(w, lead, n_i, n_f) for w, lead in cast]
    outs = pl.pallas_call(
        functools.partial(_ffn_kernel, n_f=n_f, final_norm=final_norm, n_cast=len(cast)),
        grid=(n_i, n_f),
        in_specs=in_specs + [src for src, _ in cast_specs],
        out_specs=[pl.BlockSpec((tm, d), lambda i, f: (i, 0))] + [dst for _, dst in cast_specs],
        out_shape=[jax.ShapeDtypeStruct((s, d), F32)]
        + [jax.ShapeDtypeStruct(w.shape[-2:], BF16) for w, _ in cast],
        scratch_shapes=[pltpu.VMEM((tm, d), BF16)],
        compiler_params=_params(("parallel", "arbitrary")),
        name="ffn",
    )(*args, *[w for w, _ in cast])
    return outs[0] if not cast else outs


def _norm_matmul_kernel(x_ref, g_ref, w_ref, *rest, n_cast):
    cast_in, (o_ref, *cast_out), xn_ref = rest[:n_cast], rest[n_cast:2 * n_cast + 1], rest[-1]
    j = pl.program_id(1)

    def step():
        for src, dst in zip(cast_in, cast_out):
            dst[...] = src[...].astype(BF16)
        o_ref[...] = _dot(xn_ref[...], w_ref[...]).astype(o_ref.dtype)

    @pl.when(j == 0)
    def _():
        xn_ref[...] = _rms(x_ref[...], g_ref[...]).astype(BF16)
        step()

    @pl.when(j > 0)
    def _():
        step()


def _qkv_proj_kernel(x_ref, g_ref, wr_ref, wv_ref, pos_ref, inv_ref, qk_ref, v_ref, xn_ref, tab_ref,
                     *, tiles_per_part, q_scale):
    j = pl.program_id(1)

    def prologue():
        xn_ref[...] = _rms(x_ref[...], g_ref[...]).astype(BF16)
        half = ROPE_DIM // 2
        ang = inv_ref[...] * pos_ref[...].astype(F32)
        cs, sn = jnp.cos(ang), jnp.sin(ang)
        zeros = lambda n: jnp.zeros((n, ang.shape[1]), F32)
        c = jnp.concatenate([cs, cs, jnp.ones((HEAD_DIM - 2 * half, ang.shape[1]), F32)], axis=0).T
        a = jnp.concatenate([-sn, zeros(HEAD_DIM - half)], axis=0).T
        b = jnp.concatenate([zeros(half), sn, zeros(HEAD_DIM - 2 * half)], axis=0).T
        for idx, tab in enumerate((c, a, b)):
            tab_ref[0, idx] = tab * q_scale
            tab_ref[1, idx] = tab

    def step():
        part = j // tiles_per_part
        for c0 in range(0, wr_ref.shape[1], MXU_COLS):
            y = _dot(xn_ref[...], wr_ref[:, c0:c0 + MXU_COLS])
            for h0 in range(0, MXU_COLS, HEAD_DIM):
                yh = y[:, h0:h0 + HEAD_DIM]
                up = pltpu.roll(yh, HEAD_DIM - ROPE_DIM // 2, 1)
                dn = pltpu.roll(yh, ROPE_DIM // 2, 1)
                qk_ref[:, c0 + h0:c0 + h0 + HEAD_DIM] = (
                    yh * tab_ref[part, 0] + up * tab_ref[part, 1] + dn * tab_ref[part, 2])
        v_ref[...] = _dot(xn_ref[...], wv_ref[...])

    @pl.when(j == 0)
    def _():
        prologue()
        step()

    @pl.when(j > 0)
    def _():
        step()


def _rope_inv_freq():
    freqs = ROPE_THETA ** (-np.arange(0, ROPE_DIM, 2, dtype=np.float64) / ROPE_DIM)
    return jnp.asarray(freqs.astype(np.float32).reshape(ROPE_DIM // 2, 1))


def _norm_matmul(x, g, w, cast=(), *, tm=1024, tn=PROJ_TILE):
    s, d = x.shape
    n = w.shape[1]
    tm, tn = _pick(s, tm), _pick(n, tn)
    cast_specs = [_cast_specs(cw, lead, s // tm, n // tn) for cw, lead in cast]
    in_specs = [
        pl.BlockSpec((tm, d), lambda i, j: (i, 0)),
        pl.BlockSpec((1, d), lambda i, j: (0, 0)),
        pl.BlockSpec((d, tn), lambda i, j: (0, j)),
    ]
    outs = pl.pallas_call(
        functools.partial(_norm_matmul_kernel, n_cast=len(cast)),
        grid=(s // tm, n // tn),
        in_specs=in_specs + [src for src, _ in cast_specs],
        out_specs=[pl.BlockSpec((tm, tn), lambda i, j: (i, j))] + [dst for _, dst in cast_specs],
        out_shape=[jax.ShapeDtypeStruct((s, n), BF16)]
        + [jax.ShapeDtypeStruct(cw.shape[-2:], BF16) for cw, _ in cast],
        scratch_shapes=[pltpu.VMEM((tm, d), BF16)],
        compiler_params=_params(("parallel", "arbitrary")),
        name="norm_matmul",
    )(x, g.reshape(1, d), w, *[cw for cw, _ in cast])
    return outs[0] if not cast else outs


def _qkv_proj(x, g, w, positions, *, tm=1024, rope_tile=ROPE_TILE):
    s, d = x.shape
    assert w.shape == (d, 3 * d)
    tm, rope_tile = _pick(s, tm), _pick(d, rope_tile)
    n_j = 2 * d // rope_tile
    v_tile = d // n_j
    assert rope_tile % MXU_COLS == 0 and MXU_COLS % HEAD_DIM == 0 and v_tile % HEAD_DIM == 0
    return pl.pallas_call(
        functools.partial(_qkv_proj_kernel, tiles_per_part=d // rope_tile,
                          q_scale=LOG2E / np.sqrt(HEAD_DIM)),
        grid=(s // tm, n_j),
        in_specs=[
            pl.BlockSpec((tm, d), lambda i, j: (i, 0)),
            pl.BlockSpec((1, d), lambda i, j: (0, 0)),
            pl.BlockSpec((d, rope_tile), lambda i, j: (0, j)),
            pl.BlockSpec((d, v_tile), lambda i, j: (0, 2 * d // v_tile + j)),
            pl.BlockSpec((None, 1, tm), lambda i, j: (i, 0, 0)),
            pl.BlockSpec((ROPE_DIM // 2, 1), lambda i, j: (0, 0)),
        ],
        out_specs=[pl.BlockSpec((tm, rope_tile), lambda i, j: (i, j)),
                   pl.BlockSpec((tm, v_tile), lambda i, j: (i, j))],
        out_shape=[jax.ShapeDtypeStruct((s, 2 * d), F32), jax.ShapeDtypeStruct((s, d), F32)],
        scratch_shapes=[pltpu.VMEM((tm, d), BF16), pltpu.VMEM((2, 3, tm, HEAD_DIM), F32)],
        compiler_params=_params(("parallel", "arbitrary")),
        name="qkv_proj",
    )(x, g.reshape(1, d), w, w, positions.reshape(s // tm, 1, tm), _rope_inv_freq())


def _proj_residual_kernel(a_ref, h_ref, w_ref, o_ref):
    o_ref[...] = h_ref[...] + _dot(a_ref[...], w_ref[...])


def _proj_residual(a, h, w, *, tm=512):
    s, d = h.shape
    tm = _pick(s, tm)
    row = lambda width: pl.BlockSpec((tm, width), lambda i: (i, 0))
    return pl.pallas_call(
        _proj_residual_kernel,
        grid=(s // tm,),
        in_specs=[row(a.shape[1]), row(d), pl.BlockSpec(w.shape, lambda i: (0, 0))],
        out_specs=row(d),
        out_shape=jax.ShapeDtypeStruct((s, d), F32),
        compiler_params=_params(("parallel",)),
        name="proj_residual",
    )(a, h, w)


def _mixer_kernel(zc_ref, zh_ref, h_ref, cw_ref, pw_ref, ps_ref, wo_ref, o_ref, y_ref, *, tm, cdim, pgroup):
    i = pl.program_id(0)
    has_prev = i > 0
    chunk = pgroup

    def ext(col0):
        sl = slice(col0, col0 + chunk)
        halo = jnp.where(has_prev, zh_ref[:, sl].astype(F32), 0.0)
        return jnp.concatenate([halo, zc_ref[:, sl].astype(F32)], axis=0)

    for c0 in range(0, cdim, chunk):
        cx = ext(cdim + c0) * ext(2 * cdim + c0)
        w = cw_ref[:, c0:c0 + chunk]
        conv = (w[2:3] * cx + w[1:2] * pltpu.roll(cx, 1, 0) + w[0:1] * pltpu.roll(cx, 2, 0))[HALO:]
        gate_b = zc_ref[:, c0:c0 + chunk].astype(F32)
        y_ref[:, c0:c0 + chunk] = (gate_b * conv).astype(BF16)

    t = i * tm + lax.broadcasted_iota(jnp.int32, (tm, 1), 0)
    for gi, win in enumerate(POOL_WINDOWS):
        col0 = 3 * cdim + gi * chunk
        acc = ext(col0)
        sh = 1
        while sh < win:
            acc = acc + pltpu.roll(acc, sh, 0)
            sh *= 2
        inv_cnt = 1.0 / jnp.minimum(t + 1, win).astype(F32)
        u = zc_ref[:, col0:col0 + chunk].astype(F32)
        pooled = acc[HALO:] * inv_cnt - u
        mixed = _dot(pooled.astype(BF16), pw_ref[gi])
        y_ref[:, cdim + gi * chunk: cdim + (gi + 1) * chunk] = (
            mixed * ps_ref[:, gi * chunk:(gi + 1) * chunk]).astype(BF16)

    o_ref[...] = h_ref[...] + _dot(y_ref[...], wo_ref[...])


def _mixer(z, h, conv_w, pool_w, pool_scale, w_out, *, tm=512):
    s, d = h.shape
    cdim = d // 2
    pgroup = pool_w.shape[-1]
    tm = _pick(s, tm)
    assert tm % HALO == 0 and cdim % pgroup == 0 and max(POOL_WINDOWS) <= HALO
    hb = tm // HALO
    return pl.pallas_call(
        functools.partial(_mixer_kernel, tm=tm, cdim=cdim, pgroup=pgroup),
        grid=(s // tm,),
        in_specs=[
            pl.BlockSpec((tm, 4 * cdim), lambda i: (i, 0)),
            pl.BlockSpec((HALO, 4 * cdim), lambda i: (jnp.maximum(i * hb - 1, 0), 0)),
            pl.BlockSpec((tm, d), lambda i: (i, 0)),
            pl.BlockSpec((CONV_WIDTH, cdim), lambda i: (0, 0)),
            pl.BlockSpec(pool_w.shape, lambda i: (0, 0, 0)),
            pl.BlockSpec((1, cdim), lambda i: (0, 0)),
            pl.BlockSpec((d, d), lambda i: (0, 0)),
        ],
        out_specs=pl.BlockSpec((tm, d), lambda i: (i, 0)),
        out_shape=jax.ShapeDtypeStruct((s, d), F32),
        scratch_shapes=[pltpu.VMEM((tm, d), BF16)],
        compiler_params=_params(("parallel",)),
        name="mixer",
    )(z, z, h, conv_w, pool_w, pool_scale.reshape(1, cdim), w_out)


def _attn_kernel(q_ref, k_ref, v_ref, o_ref, *scr, dils, span):
    n_br = len(dils)
    ks, vs, qs, ogs, lgs = (scr[i * n_br:(i + 1) * n_br] for i in range(5))
    t = pl.program_id(1)
    blk = ATT_BLOCK
    tile = q_ref.shape[0]

    @pl.when(t == 0)
    def _():
        for kd, vd in zip(ks, vs):
            kd[:, :blk, :] = jnp.zeros((kd.shape[0], blk, HEAD_DIM), BF16)
            vd[:, :blk, :] = jnp.zeros((vd.shape[0], blk, HEAD_DIM), BF16)

    @pl.when(t > 0)
    def _():
        for kd, vd, d in zip(ks, vs, dils):
            n = tile // d
            kd[:, :blk, :] = kd[:, n:n + blk, :]
            vd[:, :blk, :] = vd[:, n:n + blk, :]

    for kd, vd, qd, d in zip(ks, vs, qs, dils):
        n = tile // d
        for r in range(d):
            rows = pl.ds(r, n, stride=d) if d > 1 else slice(None)
            kd[r, blk:, :] = k_ref[rows, :].astype(BF16)
            vd[r, blk:, :] = v_ref[rows, :].astype(BF16)
            qd[r] = q_ref[rows, :].astype(BF16)

    lm = lax.broadcasted_iota(jnp.int32, (8, blk), 1) - lax.broadcasted_iota(jnp.int32, (8, blk), 0)
    never = jnp.where(t == 0, 2 * blk, 0)

    def masked(s, first):
        slabs = []
        for j8 in range(0, blk, 8):
            lo = j8 + blk - span
            prev = jnp.where(lm >= (lo + never if first else lo), s[j8:j8 + 8, :blk], NEG_INF)
            cur = jnp.where(lm <= j8, s[j8:j8 + 8, blk:], NEG_INF)
            slabs.append(jnp.concatenate([prev, cur], axis=1))
        return jnp.concatenate(slabs, axis=0)

    for g, d in enumerate(dils):
        n = tile // d
        for r in range(d):
            for b in range(n // blk):
                q = qs[g][r, b * blk:(b + 1) * blk, :]
                kk = ks[g][r, b * blk:(b + 2) * blk, :]
                vv = vs[g][r, b * blk:(b + 2) * blk, :]
                s = lax.dot_general(q, kk, (((1,), (1,)), ((), ())), preferred_element_type=F32)
                s = masked(s, b == 0)
                m = jnp.max(s, axis=-1, keepdims=True)
                p = jnp.exp2(s - m)
                l = jnp.sum(p, axis=-1, keepdims=True)
                o = _dot(p.astype(BF16), vv) / l
                lse = m * LN2 + jnp.log(l)
                rows = pl.ds(b * blk * d + r, blk, stride=d) if d > 1 else slice(b * blk, (b + 1) * blk)
                ogs[g][rows, :] = o
                lgs[g][rows, :] = jnp.broadcast_to(lse, (blk, HEAD_DIM))

    for c0 in range(0, tile, COMBINE_ROWS):
        rows = slice(c0, c0 + COMBINE_ROWS)
        ls = [lg[rows, :] for lg in lgs]
        m = functools.reduce(jnp.maximum, ls)
        es = [jnp.exp(x - m) for x in ls]
        tot = functools.reduce(lambda x, y: x + y, es)
        acc = functools.reduce(lambda x, y: x + y, [e * og[rows, :] for e, og in zip(es, ogs)])
        o_ref[rows, :] = (acc / tot).astype(o_ref.dtype)


def _attention(qk, v, *, tile=ATT_TILE):
    s, d_model = v.shape
    n_heads = d_model // HEAD_DIM
    dils = tuple(d for _, d in DIL_BRANCHES)
    spans = {w // d for w, d in DIL_BRANCHES}
    assert len(spans) == 1 and s % tile == 0 and all(tile % (ATT_BLOCK * d) == 0 for d in dils)
    span = spans.pop()
    assert ATT_BLOCK - 1 <= span <= ATT_BLOCK
    kv = [pltpu.VMEM((d, ATT_BLOCK + tile // d, HEAD_DIM), BF16) for d in dils]
    qd = [pltpu.VMEM((d, tile // d, HEAD_DIM), BF16) for d in dils]
    acc = [pltpu.VMEM((tile, HEAD_DIM), F32) for _ in dils]
    blockspec = lambda part: pl.BlockSpec((tile, HEAD_DIM), lambda h, t: (t, part * n_heads + h))
    return pl.pallas_call(
        functools.partial(_attn_kernel, dils=dils, span=span),
        grid=(n_heads, s // tile),
        in_specs=[blockspec(0), blockspec(1), blockspec(0)],
        out_specs=pl.BlockSpec((tile, HEAD_DIM), lambda h, t: (t, h)),
        out_shape=jax.ShapeDtypeStruct((s, d_model), BF16),
        scratch_shapes=kv + kv + qd + acc + acc,
        compiler_params=_params(("parallel", "arbitrary")),
        name="attn",
    )(qk, qk, v)


def kernel(x, positions, norm_g, ffn_w1, ffn_w3, ffn_w2, mix_w_in, conv_w, pool_w, pool_scale,
           mix_w_out, attn_w_qkv, attn_w_o, final_g):
    b, s, d = x.shape
    depth = norm_g.shape[0]
    bf = lambda w: w.astype(BF16)
    ffns = [(i, j) for i in range(depth) for j in range(2)]
    wts = {ffns[0]: [bf(w[ffns[0]]) for w in (ffn_w1, ffn_w3, ffn_w2)]}

    def ffn(h, key, g, final_g=None):
        nxt = ffns.index(key) + 1
        if nxt < len(ffns) and ffns[nxt] not in wts:
            h, *wts[ffns[nxt]] = _ffn(h, g, *wts[key], final_g=final_g,
                                      cast=[(w, ffns[nxt]) for w in (ffn_w1, ffn_w3, ffn_w2)])
            return h
        return _ffn(h, g, *wts[key], final_g=final_g)

    proj_w = {}
    outs = []
    for bi in range(b):
        h = x[bi]
        for i in range(depth):
            h = ffn(h, (i, 0), norm_g[i, 0])
            j = i // 2
            if i % 2 == 0:
                later = {"mix_out": (mix_w_out, (j,))}
                if i + 1 < depth:
                    later.update(qkv=(attn_w_qkv, (j,)), attn_out=(attn_w_o, (j,)))
                later = {k: v for k, v in later.items() if (k, v[1]) not in proj_w}
                res = _norm_matmul(h, norm_g[i, 1], bf(mix_w_in[j]), cast=list(later.values()))
                z, cast_w = (res[0], res[1:]) if later else (res, [])
                proj_w.update({(k, v[1]): cw for (k, v), cw in zip(later.items(), cast_w)})
                h = _mixer(z, h, conv_w[j], bf(pool_w[j]), pool_scale[j], proj_w["mix_out", (j,)])
            else:
                w_qkv = proj_w.get(("qkv", (j,)))
                w_o = proj_w.get(("attn_out", (j,)))
                w_qkv = bf(attn_w_qkv[j]) if w_qkv is None else w_qkv
                w_o = bf(attn_w_o[j]) if w_o is None else w_o
                qk, v = _qkv_proj(h, norm_g[i, 1], w_qkv, positions[bi])
                h = _proj_residual(_attention(qk, v), h, w_o)
            h = ffn(h, (i, 1), norm_g[i, 2], final_g=final_g if i == depth - 1 else None)
        outs.append(h.reshape(1, s, d))
    return outs[0] if b == 1 else jnp.concatenate(outs, axis=0)
```

```python
import functools

import numpy as np
import jax
import jax.numpy as jnp
from jax import lax
from jax.experimental import pallas as pl
from jax.experimental.pallas import tpu as pltpu

F32 = jnp.float32
BF16 = jnp.bfloat16

HEAD_DIM = 128
ROPE_DIM = HEAD_DIM // 4
ROPE_THETA = 500000.0
DIL_BRANCHES = ((128, 1), (512, 4), (2048, 16))
ATT_BLOCK = 128
ATT_TILE = 2 * ATT_BLOCK * max(d for _, d in DIL_BRANCHES)
COMBINE_ROWS = 256
EPILOGUE_ROWS = 256
CONV_WIDTH = 3
POOL_WINDOWS = (2, 4, 8, 16)
HALO = 16
RMS_EPS = 1e-6
NEG_INF = -1e30
LOG2E = float(np.log2(np.e))
LN2 = float(np.log(2.0))

MXU_COLS = 256
ROPE_TILE = 1024
PROJ_TILE = 2048
V7X_VMEM_BYTES = 64 * 1024 * 1024
VMEM_LIMIT = V7X_VMEM_BYTES - 6 * 1024 * 1024


def _params(sem):
    return pltpu.CompilerParams(dimension_semantics=sem, vmem_limit_bytes=VMEM_LIMIT)


def _rms(x, g):
    ms = jnp.mean(x * x, axis=-1, keepdims=True)
    return x * lax.rsqrt(ms + RMS_EPS) * g


def _dot(a, b):
    return jnp.dot(a, b, preferred_element_type=F32)


def _pick(n, pref):
    t = min(n, pref)
    while n % t:
        t //= 2
    return t


def _ffn_kernel(x_ref, g_ref, w1_ref, w3_ref, w2_ref, *rest, n_f, final_norm, n_cast):
    rest = list(rest)
    fg_ref = rest.pop(0) if final_norm else None
    cast_in = [rest.pop(0) for _ in range(n_cast)]
    o_ref = rest.pop(0)
    cast_out = [rest.pop(0) for _ in range(n_cast)]
    (xn_ref,) = rest
    f = pl.program_id(1)

    def step():
        xn = xn_ref[...]
        h1 = _dot(xn, w1_ref[...])
        h3 = _dot(xn, w3_ref[...])
        a = (h1 * jax.nn.sigmoid(h1) * (0.5 * h3)).astype(BF16)
        o_ref[...] += _dot(a, w2_ref[...])
        for src, dst in zip(cast_in, cast_out):
            dst[...] = src[...].astype(BF16)

    @pl.when(f == 0)
    def _():
        xn_ref[...] = _rms(x_ref[...], g_ref[...]).astype(BF16)
        o_ref[...] = x_ref[...]
        step()

    @pl.when(f > 0)
    def _():
        step()

    if final_norm:
        @pl.when(f == n_f - 1)
        def _():
            def chunk(c, carry):
                rows = pl.ds(pl.multiple_of(c * EPILOGUE_ROWS, EPILOGUE_ROWS), EPILOGUE_ROWS)
                o_ref[rows, :] = _rms(o_ref[rows, :], fg_ref[...])
                return carry

            lax.fori_loop(0, o_ref.shape[0] // EPILOGUE_ROWS, chunk, 0)


def _cast_specs(w, lead, n_i, n_f):
    r, c = w.shape[-2:]
    wide = c >= r
    shape = (r // n_i, c // n_f) if wide else (r // n_f, c // n_i)
    assert shape[0] * (n_i if wide else n_f) == r and shape[1] * (n_f if wide else n_i) == c
    assert shape[0] % 16 == 0 and shape[1] % 128 == 0, shape
    block = (lambda i, f: (i, f)) if wide else (lambda i, f: (f, i))
    src = pl.BlockSpec((None,) * len(lead) + shape, lambda i, f: tuple(lead) + block(i, f))
    return src, pl.BlockSpec(shape, block)


def _ffn(x, g, w1, w3, w2, final_g=None, cast=(), *, tm=1024, tf=512):
    s, d = x.shape
    ff = w1.shape[-1]
    tm, tf = _pick(s, tm), _pick(ff, tf)
    n_i, n_f = s // tm, ff // tf
    final_norm = final_g is not None
    in_specs = [
        pl.BlockSpec((tm, d), lambda i, f: (i, 0)),
        pl.BlockSpec((1, d), lambda i, f: (0, 0)),
        pl.BlockSpec((d, tf), lambda i, f: (0, f)),
        pl.BlockSpec((d, tf), lambda i, f: (0, f)),
        pl.BlockSpec((tf, d), lambda i, f: (f, 0)),
    ]
    args = [x, g.reshape(1, d), w1, w3, w2]
    if final_norm:
        in_specs.append(pl.BlockSpec((1, d), lambda i, f: (0, 0)))
        args.append(final_g.reshape(1, d))
    cast_specs = [_cast_specs(w, lead, n_i, n_f) for w, lead in cast]
    outs = pl.pallas_call(
        functools.partial(_ffn_kernel, n_f=n_f, final_norm=final_norm, n_cast=len(cast)),
        grid=(n_i, n_f),
        in_specs=in_specs + [src for src, _ in cast_specs],
        out_specs=[pl.BlockSpec((tm, d), lambda i, f: (i, 0))] + [dst for _, dst in cast_specs],
        out_shape=[jax.ShapeDtypeStruct((s, d), F32)]
        + [jax.ShapeDtypeStruct(w.shape[-2:], BF16) for w, _ in cast],
        scratch_shapes=[pltpu.VMEM((tm, d), BF16)],
        compiler_params=_params(("parallel", "arbitrary")),
        name="ffn",
    )(*args, *[w for w, _ in cast])
    return outs[0] if not cast else outs


def _norm_matmul_kernel(x_ref, g_ref, w_ref, *rest, n_cast):
    cast_in, (o_ref, *cast_out), xn_ref = rest[:n_cast], rest[n_cast:2 * n_cast + 1], rest[-1]
    j = pl.program_id(1)

    def step():
        for src, dst in zip(cast_in, cast_out):
            dst[...] = src[...].astype(BF16)
        o_ref[...] = _dot(xn_ref[...], w_ref[...]).astype(o_ref.dtype)

    @pl.when(j == 0)
    def _():
        xn_ref[...] = _rms(x_ref[...], g_ref[...]).astype(BF16)
        step()

    @pl.when(j > 0)
    def _():
        step()


def _qkv_proj_kernel(x_ref, g_ref, wr_ref, wv_ref, pos_ref, inv_ref, qk_ref, v_ref, xn_ref, tab_ref,
                     *, tiles_per_part, q_scale):
    j = pl.program_id(1)

    def prologue():
        xn_ref[...] = _rms(x_ref[...], g_ref[...]).astype(BF16)
        half = ROPE_DIM // 2
        ang = inv_ref[...] * pos_ref[...].astype(F32)
        cs, sn = jnp.cos(ang), jnp.sin(ang)
        zeros = lambda n: jnp.zeros((n, ang.shape[1]), F32)
        c = jnp.concatenate([cs, cs, jnp.ones((HEAD_DIM - 2 * half, ang.shape[1]), F32)], axis=0).T
        a = jnp.concatenate([-sn, zeros(HEAD_DIM - half)], axis=0).T
        b = jnp.concatenate([zeros(half), sn, zeros(HEAD_DIM - 2 * half)], axis=0).T
        for idx, tab in enumerate((c, a, b)):
            tab_ref[0, idx] = tab * q_scale
            tab_ref[1, idx] = tab

    def step():
        part = j // tiles_per_part
        for c0 in range(0, wr_ref.shape[1], MXU_COLS):
            y = _dot(xn_ref[...], wr_ref[:, c0:c0 + MXU_COLS])
            for h0 in range(0, MXU_COLS, HEAD_DIM):
                yh = y[:, h0:h0 + HEAD_DIM]
                up = pltpu.roll(yh, HEAD_DIM - ROPE_DIM // 2, 1)
                dn = pltpu.roll(yh, ROPE_DIM // 2, 1)
                qk_ref[:, c0 + h0:c0 + h0 + HEAD_DIM] = (
                    yh * tab_ref[part, 0] + up * tab_ref[part, 1] + dn * tab_ref[part, 2])
        v_ref[...] = _dot(xn_ref[...], wv_ref[...])

    @pl.when(j == 0)
    def _():
        prologue()
        step()

    @pl.when(j > 0)
    def _():
        step()


def _rope_inv_freq():
    freqs = ROPE_THETA ** (-np.arange(0, ROPE_DIM, 2, dtype=np.float64) / ROPE_DIM)
    return jnp.asarray(freqs.astype(np.float32).reshape(ROPE_DIM // 2, 1))


def _norm_matmul(x, g, w, cast=(), *, tm=1024, tn=PROJ_TILE):
    s, d = x.shape
    n = w.shape[1]
    tm, tn = _pick(s, tm), _pick(n, tn)
    cast_specs = [_cast_specs(cw, lead, s // tm, n // tn) for cw, lead in cast]
    in_specs = [
        pl.BlockSpec((tm, d), lambda i, j: (i, 0)),
        pl.BlockSpec((1, d), lambda i, j: (0, 0)),
        pl.BlockSpec((d, tn), lambda i, j: (0, j)),
    ]
    outs = pl.pallas_call(
        functools.partial(_norm_matmul_kernel, n_cast=len(cast)),
        grid=(s // tm, n // tn),
        in_specs=in_specs + [src for src, _ in cast_specs],
        out_specs=[pl.BlockSpec((tm, tn), lambda i, j: (i, j))] + [dst for _, dst in cast_specs],
        out_shape=[jax.ShapeDtypeStruct((s, n), BF16)]
        + [jax.ShapeDtypeStruct(cw.shape[-2:], BF16) for cw, _ in cast],
        scratch_shapes=[pltpu.VMEM((tm, d), BF16)],
        compiler_params=_params(("parallel", "arbitrary")),
        name="norm_matmul",
    )(x, g.reshape(1, d), w, *[cw for cw, _ in cast])
    return outs[0] if not cast else outs


def _qkv_proj(x, g, w, positions, *, tm=1024, rope_tile=ROPE_TILE):
    s, d = x.shape
    assert w.shape == (d, 3 * d)
    tm, rope_tile = _pick(s, tm), _pick(d, rope_tile)
    n_j = 2 * d // rope_tile
    v_tile = d // n_j
    assert rope_tile % MXU_COLS == 0 and MXU_COLS % HEAD_DIM == 0 and v_tile % HEAD_DIM == 0
    return pl.pallas_call(
        functools.partial(_qkv_proj_kernel, tiles_per_part=d // rope_tile,
                          q_scale=LOG2E / np.sqrt(HEAD_DIM)),
        grid=(s // tm, n_j),
        in_specs=[
            pl.BlockSpec((tm, d), lambda i, j: (i, 0)),
            pl.BlockSpec((1, d), lambda i, j: (0, 0)),
            pl.BlockSpec((d, rope_tile), lambda i, j: (0, j)),
            pl.BlockSpec((d, v_tile), lambda i, j: (0, 2 * d // v_tile + j)),
            pl.BlockSpec((None, 1, tm), lambda i, j: (i, 0, 0)),
            pl.BlockSpec((ROPE_DIM // 2, 1), lambda i, j: (0, 0)),
        ],
        out_specs=[pl.BlockSpec((tm, rope_tile), lambda i, j: (i, j)),
                   pl.BlockSpec((tm, v_tile), lambda i, j: (i, j))],
        out_shape=[jax.ShapeDtypeStruct((s, 2 * d), F32), jax.ShapeDtypeStruct((s, d), F32)],
        scratch_shapes=[pltpu.VMEM((tm, d), BF16), pltpu.VMEM((2, 3, tm, HEAD_DIM), F32)],
        compiler_params=_params(("parallel", "arbitrary")),
        name="qkv_proj",
    )(x, g.reshape(1, d), w, w, positions.reshape(s // tm, 1, tm), _rope_inv_freq())


def _proj_residual_kernel(a_ref, h_ref, w_ref, o_ref):
    o_ref[...] = h_ref[...] + _dot(a_ref[...], w_ref[...])


def _proj_residual(a, h, w, *, tm=512):
    s, d = h.shape
    tm = _pick(s, tm)
    row = lambda width: pl.BlockSpec((tm, width), lambda i: (i, 0))
    return pl.pallas_call(
        _proj_residual_kernel,
        grid=(s // tm,),
        in_specs=[row(a.shape[1]), row(d), pl.BlockSpec(w.shape, lambda i: (0, 0))],
        out_specs=row(d),
        out_shape=jax.ShapeDtypeStruct((s, d), F32),
        compiler_params=_params(("parallel",)),
        name="proj_residual",
    )(a, h, w)


def _mixer_kernel(zc_ref, zh_ref, h_ref, cw_ref, pw_ref, ps_ref, wo_ref, o_ref, y_ref, *, tm, cdim, pgroup):
    i = pl.program_id(0)
    has_prev = i > 0
    chunk = pgroup

    def ext(col0):
        sl = slice(col0, col0 + chunk)
        halo = jnp.where(has_prev, zh_ref[:, sl].astype(F32), 0.0)
        return jnp.concatenate([halo, zc_ref[:, sl].astype(F32)], axis=0)

    for c0 in range(0, cdim, chunk):
        cx = ext(cdim + c0) * ext(2 * cdim + c0)
        w = cw_ref[:, c0:c0 + chunk]
        conv = (w[2:3] * cx + w[1:2] * pltpu.roll(cx, 1, 0) + w[0:1] * pltpu.roll(cx, 2, 0))[HALO:]
        gate_b = zc_ref[:, c0:c0 + chunk].astype(F32)
        y_ref[:, c0:c0 + chunk] = (gate_b * conv).astype(BF16)

    t = i * tm + lax.broadcasted_iota(jnp.int32, (tm, 1), 0)
    for gi, win in enumerate(POOL_WINDOWS):
        col0 = 3 * cdim + gi * chunk
        acc = ext(col0)
        sh = 1
        while sh < win:
            acc = acc + pltpu.roll(acc, sh, 0)
            sh *= 2
        inv_cnt = 1.0 / jnp.minimum(t + 1, win).astype(F32)
        u = zc_ref[:, col0:col0 + chunk].astype(F32)
        pooled = acc[HALO:] * inv_cnt - u
        mixed = _dot(pooled.astype(BF16), pw_ref[gi])
        y_ref[:, cdim + gi * chunk: cdim + (gi + 1) * chunk] = (
            mixed * ps_ref[:, gi * chunk:(gi + 1) * chunk]).astype(BF16)

    o_ref[...] = h_ref[...] + _dot(y_ref[...], wo_ref[...])


def _mixer(z, h, conv_w, pool_w, pool_scale, w_out, *, tm=512):
    s, d = h.shape
    cdim = d // 2
    pgroup = pool_w.shape[-1]
    tm = _pick(s, tm)
    assert tm % HALO == 0 and cdim % pgroup == 0 and max(POOL_WINDOWS) <= HALO
    hb = tm // HALO
    return pl.pallas_call(
        functools.partial(_mixer_kernel, tm=tm, cdim=cdim, pgroup=pgroup),
        grid=(s // tm,),
        in_specs=[
            pl.BlockSpec((tm, 4 * cdim), lambda i: (i, 0)),
            pl.BlockSpec((HALO, 4 * cdim), lambda i: (jnp.maximum(i * hb - 1, 0), 0)),
            pl.BlockSpec((tm, d), lambda i: (i, 0)),
            pl.BlockSpec((CONV_WIDTH, cdim), lambda i: (0, 0)),
            pl.BlockSpec(pool_w.shape, lambda i: (0, 0, 0)),
            pl.BlockSpec((1, cdim), lambda i: (0, 0)),
            pl.BlockSpec((d, d), lambda i: (0, 0)),
        ],
        out_specs=pl.BlockSpec((tm, d), lambda i: (i, 0)),
        out_shape=jax.ShapeDtypeStruct((s, d), F32),
        scratch_shapes=[pltpu.VMEM((tm, d), BF16)],
        compiler_params=_params(("parallel",)),
        name="mixer",
    )(z, z, h, conv_w, pool_w, pool_scale.reshape(1, cdim), w_out)


def _attn_kernel(q_ref, k_ref, v_ref, o_ref, *scr, dils, span):
    n_br = len(dils)
    ks, vs, qs, ogs, lgs = (scr[i * n_br:(i + 1) * n_br] for i in range(5))
    t = pl.program_id(1)
    blk = ATT_BLOCK
    tile = q_ref.shape[0]

    @pl.when(t == 0)
    def _():
        for kd, vd in zip(ks, vs):
            kd[:, :blk, :] = jnp.zeros((kd.shape[0], blk, HEAD_DIM), BF16)
            vd[:, :blk, :] = jnp.zeros((vd.shape[0], blk, HEAD_DIM), BF16)

    @pl.when(t > 0)
    def _():
        for kd, vd, d in zip(ks, vs, dils):
            n = tile // d
            kd[:, :blk, :] = kd[:, n:n + blk, :]
            vd[:, :blk, :] = vd[:, n:n + blk, :]

    for kd, vd, qd, d in zip(ks, vs, qs, dils):
        n = tile // d
        for r in range(d):
            rows = pl.ds(r, n, stride=d) if d > 1 else slice(None)
            kd[r, blk:, :] = k_ref[rows, :].astype(BF16)
            vd[r, blk:, :] = v_ref[rows, :].astype(BF16)
            qd[r] = q_ref[rows, :].astype(BF16)

    lm = lax.broadcasted_iota(jnp.int32, (8, blk), 1) - lax.broadcasted_iota(jnp.int32, (8, blk), 0)
    never = jnp.where(t == 0, 2 * blk, 0)

    def masked(s, first):
        slabs = []
        for j8 in range(0, blk, 8):
            lo = j8 + blk - span
            prev = jnp.where(lm >= (lo + never if first else lo), s[j8:j8 + 8, :blk], NEG_INF)
            cur = jnp.where(lm <= j8, s[j8:j8 + 8, blk:], NEG_INF)
            slabs.append(jnp.concatenate([prev, cur], axis=1))
        return jnp.concatenate(slabs, axis=0)

    for g, d in enumerate(dils):
        n = tile // d
        for r in range(d):
            for b in range(n // blk):
                q = qs[g][r, b * blk:(b + 1) * blk, :]
                kk = ks[g][r, b * blk:(b + 2) * blk, :]
                vv = vs[g][r, b * blk:(b + 2) * blk, :]
                s = lax.dot_general(q, kk, (((1,), (1,)), ((), ())), preferred_element_type=F32)
                s = masked(s, b == 0)
                m = jnp.max(s, axis=-1, keepdims=True)
                p = jnp.exp2(s - m)
                l = jnp.sum(p, axis=-1, keepdims=True)
                o = _dot(p.astype(BF16), vv) / l
                lse = m * LN2 + jnp.log(l)
                rows = pl.ds(b * blk * d + r, blk, stride=d) if d > 1 else slice(b * blk, (b + 1) * blk)
                ogs[g][rows, :] = o
                lgs[g][rows, :] = jnp.broadcast_to(lse, (blk, HEAD_DIM))

    for c0 in range(0, tile, COMBINE_ROWS):
        rows = slice(c0, c0 + COMBINE_ROWS)
        ls = [lg[rows, :] for lg in lgs]
        m = functools.reduce(jnp.maximum, ls)
        es = [jnp.exp(x - m) for x in ls]
        tot = functools.reduce(lambda x, y: x + y, es)
        acc = functools.reduce(lambda x, y: x + y, [e * og[rows, :] for e, og in zip(es, ogs)])
        o_ref[rows, :] = (acc / tot).astype(o_ref.dtype)


def _attention(qk, v, *, tile=ATT_TILE):
    s, d_model = v.shape
    n_heads = d_model // HEAD_DIM
    dils = tuple(d for _, d in DIL_BRANCHES)
    spans = {w // d for w, d in DIL_BRANCHES}
    assert len(spans) == 1 and s % tile == 0 and all(tile % (ATT_BLOCK * d) == 0 for d in dils)
    span = spans.pop()
    assert ATT_BLOCK - 1 <= span <= ATT_BLOCK
    kv = [pltpu.VMEM((d, ATT_BLOCK + tile // d, HEAD_DIM), BF16) for d in dils]
    qd = [pltpu.VMEM((d, tile // d, HEAD_DIM), BF16) for d in dils]
    acc = [pltpu.VMEM((tile, HEAD_DIM), F32) for _ in dils]
    blockspec = lambda part: pl.BlockSpec((tile, HEAD_DIM), lambda h, t: (t, part * n_heads + h))
    return pl.pallas_call(
        functools.partial(_attn_kernel, dils=dils, span=span),
        grid=(n_heads, s // tile),
        in_specs=[blockspec(0), blockspec(1), blockspec(0)],
        out_specs=pl.BlockSpec((tile, HEAD_DIM), lambda h, t: (t, h)),
        out_shape=jax.ShapeDtypeStruct((s, d_model), BF16),
        scratch_shapes=kv + kv + qd + acc + acc,
        compiler_params=_params(("parallel", "arbitrary")),
        name="attn",
    )(qk, qk, v)


def kernel(x, positions, norm_g, ffn_w1, ffn_w3, ffn_w2, mix_w_in, conv_w, pool_w, pool_scale,
           mix_w_out, attn_w_qkv, attn_w_o, final_g):
    b, s, d = x.shape
    depth = norm_g.shape[0]
    bf = lambda w: w.astype(BF16)
    ffns = [(i, j) for i in range(depth) for j in range(2)]
    wts = {ffns[0]: [bf(w[ffns[0]]) for w in (ffn_w1, ffn_w3, ffn_w2)]}

    def ffn(h, key, g, final_g=None):
        nxt = ffns.index(key) + 1
        if nxt < len(ffns) and ffns[nxt] not in wts:
            h, *wts[ffns[nxt]] = _ffn(h, g, *wts[key], final_g=final_g,
                                      cast=[(w, ffns[nxt]) for w in (ffn_w1, ffn_w3, ffn_w2)])
            return h
        return _ffn(h, g, *wts[key], final_g=final_g)

    proj_w = {}
    outs = []
    for bi in range(b):
        h = x[bi]
        for i in range(depth):
            h = ffn(h, (i, 0), norm_g[i, 0])
            j = i // 2
            if i % 2 == 0:
                later = {"mix_out": (mix_w_out, (j,))}
                if i + 1 < depth:
                    later.update(qkv=(attn_w_qkv, (j,)), attn_out=(attn_w_o, (j,)))
                later = {k: v for k, v in later.items() if (k, v[1]) not in proj_w}
                res = _norm_matmul(h, norm_g[i, 1], bf(mix_w_in[j]), cast=list(later.values()))
                z, cast_w = (res[0], res[1:]) if later else (res, [])
                proj_w.update({(k, v[1]): cw for (k, v), cw in zip(later.items(), cast_w)})
                h = _mixer(z, h, conv_w[j], bf(pool_w[j]), pool_scale[j], proj_w["mix_out", (j,)])
            else:
                w_qkv = proj_w.get(("qkv", (j,)))
                w_o = proj_w.get(("attn_out", (j,)))
                w_qkv = bf(attn_w_qkv[j]) if w_qkv is None else w_qkv
                w_o = bf(attn_w_o[j]) if w_o is None else w_o
                qk, v = _qkv_proj(h, norm_g[i, 1], w_qkv, positions[bi])
                h = _proj_residual(_attention(qk, v), h, w_o)
            h = ffn(h, (i, 1), norm_g[i, 2], final_g=final_g if i == depth - 1 else None)
        outs.append(h.reshape(1, s, d))
    return outs[0] if b == 1 else jnp.concatenate(outs, axis=0)
```

```python
import functools

import numpy as np
import jax
import jax.numpy as jnp
from jax import lax
from jax.experimental import pallas as pl
from jax.experimental.pallas import tpu as pltpu

F32 = jnp.float32
BF16 = jnp.bfloat16

HEAD_DIM = 128
ROPE_DIM = HEAD_DIM // 4
ROPE_THETA = 500000.0
DIL_BRANCHES = ((128, 1), (512, 4), (2048, 16))
ATT_BLOCK = 128
ATT_TILE = 2 * ATT_BLOCK * max(d for _, d in DIL_BRANCHES)
COMBINE_ROWS = 256
EPILOGUE_ROWS = 256
CONV_WIDTH = 3
POOL_WINDOWS = (2, 4, 8, 16)
HALO = 16
RMS_EPS = 1e-6
NEG_INF = -1e30
LOG2E = float(np.log2(np.e))
LN2 = float(np.log(2.0))

MXU_COLS = 256
ROPE_TILE = 1024
PROJ_TILE = 2048
V7X_VMEM_BYTES = 64 * 1024 * 1024
VMEM_LIMIT = V7X_VMEM_BYTES - 6 * 1024 * 1024


def _params(sem):
    return pltpu.CompilerParams(dimension_semantics=sem, vmem_limit_bytes=VMEM_LIMIT)


def _rms(x, g):
    ms = jnp.mean(x * x, axis=-1, keepdims=True)
    return x * lax.rsqrt(ms + RMS_EPS) * g


def _dot(a, b):
    return jnp.dot(a, b, preferred_element_type=F32)


def _pick(n, pref):
    t = min(n, pref)
    while n % t:
        t //= 2
    return t


def _ffn_kernel(x_ref, g_ref, w1_ref, w3_ref, w2_ref, *rest, n_f, final_norm, n_cast):
    rest = list(rest)
    fg_ref = rest.pop(0) if final_norm else None
    cast_in = [rest.pop(0) for _ in range(n_cast)]
    o_ref = rest.pop(0)
    cast_out = [rest.pop(0) for _ in range(n_cast)]
    (xn_ref,) = rest
    f = pl.program_id(1)

    def step():
        xn = xn_ref[...]
        h1 = _dot(xn, w1_ref[...])
        h3 = _dot(xn, w3_ref[...])
        a = (h1 * jax.nn.sigmoid(h1) * (0.5 * h3)).astype(BF16)
        o_ref[...] += _dot(a, w2_ref[...])
        for src, dst in zip(cast_in, cast_out):
            dst[...] = src[...].astype(BF16)

    @pl.when(f == 0)
    def _():
        xn_ref[...] = _rms(x_ref[...], g_ref[...]).astype(BF16)
        o_ref[...] = x_ref[...]
        step()

    @pl.when(f > 0)
    def _():
        step()

    if final_norm:
        @pl.when(f == n_f - 1)
        def _():
            def chunk(c, carry):
                rows = pl.ds(pl.multiple_of(c * EPILOGUE_ROWS, EPILOGUE_ROWS), EPILOGUE_ROWS)
                o_ref[rows, :] = _rms(o_ref[rows, :], fg_ref[...])
                return carry

            lax.fori_loop(0, o_ref.shape[0] // EPILOGUE_ROWS, chunk, 0)


def _cast_specs(w, lead, n_i, n_f):
    r, c = w.shape[-2:]
    wide = c >= r
    shape = (r // n_i, c // n_f) if wide else (r // n_f, c // n_i)
    assert shape[0] * (n_i if wide else n_f) == r and shape[1] * (n_f if wide else n_i) == c
    assert shape[0] % 16 == 0 and shape[1] % 128 == 0, shape
    block = (lambda i, f: (i, f)) if wide else (lambda i, f: (f, i))
    src = pl.BlockSpec((None,) * len(lead) + shape, lambda i, f: tuple(lead) + block(i, f))
    return src, pl.BlockSpec(shape, block)


def _ffn(x, g, w1, w3, w2, final_g=None, cast=(), *, tm=1024, tf=512):
    s, d = x.shape
    ff = w1.shape[-1]
    tm, tf = _pick(s, tm), _pick(ff, tf)
    n_i, n_f = s // tm, ff // tf
    final_norm = final_g is not None
    in_specs = [
        pl.BlockSpec((tm, d), lambda i, f: (i, 0)),
        pl.BlockSpec((1, d), lambda i, f: (0, 0)),
        pl.BlockSpec((d, tf), lambda i, f: (0, f)),
        pl.BlockSpec((d, tf), lambda i, f: (0, f)),
        pl.BlockSpec((tf, d), lambda i, f: (f, 0)),
    ]
    args = [x, g.reshape(1, d), w1, w3, w2]
    if final_norm:
        in_specs.append(pl.BlockSpec((1, d), lambda i, f: (0, 0)))
        args.append(final_g.reshape(1, d))
    cast_specs = [_cast_specs(w, lead, n_i, n_f) for w, lead in cast]
    outs = pl.pallas_call(
        functools.partial(_ffn_kernel, n_f=n_f, final_norm=final_norm, n_cast=len(cast)),
        grid=(n_i, n_f),
        in_specs=in_specs + [src for src, _ in cast_specs],
        out_specs=[pl.BlockSpec((tm, d), lambda i, f: (i, 0))] + [dst for _, dst in cast_specs],
        out_shape=[jax.ShapeDtypeStruct((s, d), F32)]
        + [jax.ShapeDtypeStruct(w.shape[-2:], BF16) for w, _ in cast],
        scratch_shapes=[pltpu.VMEM((tm, d), BF16)],
        compiler_params=_params(("parallel", "arbitrary")),
        name="ffn",
    )(*args, *[w for w, _ in cast])
    return outs[0] if not cast else outs


def _norm_matmul_kernel(x_ref, g_ref, w_ref, *rest, n_cast):
    cast_in, (o_ref, *cast_out), xn_ref = rest[:n_cast], rest[n_cast:2 * n_cast + 1], rest[-1]
    j = pl.program_id(1)

    def step():
        for src, dst in zip(cast_in, cast_out):
            dst[...] = src[...].astype(BF16)
        o_ref[...] = _dot(xn_ref[...], w_ref[...]).astype(o_ref.dtype)

    @pl.when(j == 0)
    def _():
        xn_ref[...] = _rms(x_ref[...], g_ref[...]).astype(BF16)
        step()

    @pl.when(j > 0)
    def _():
        step()


def _qkv_proj_kernel(x_ref, g_ref, wr_ref, wv_ref, pos_ref, inv_ref, qk_ref, v_ref, xn_ref, tab_ref,
                     *, tiles_per_part, q_scale):
    j = pl.program_id(1)

    def prologue():
        xn_ref[...] = _rms(x_ref[...], g_ref[...]).astype(BF16)
        half = ROPE_DIM // 2
        ang = inv_ref[...] * pos_ref[...].astype(F32)
        cs, sn = jnp.cos(ang), jnp.sin(ang)
        zeros = lambda n: jnp.zeros((n, ang.shape[1]), F32)
        c = jnp.concatenate([cs, cs, jnp.ones((HEAD_DIM - 2 * half, ang.shape[1]), F32)], axis=0).T
        a = jnp.concatenate([-sn, zeros(HEAD_DIM - half)], axis=0).T
        b = jnp.concatenate([zeros(half), sn, zeros(HEAD_DIM - 2 * half)], axis=0).T
        for idx, tab in enumerate((c, a, b)):
            tab_ref[0, idx] = tab * q_scale
            tab_ref[1, idx] = tab

    def step():
        part = j // tiles_per_part
        for c0 in range(0, wr_ref.shape[1], MXU_COLS):
            y = _dot(xn_ref[...], wr_ref[:, c0:c0 + MXU_COLS])
            for h0 in range(0, MXU_COLS, HEAD_DIM):
                yh = y[:, h0:h0 + HEAD_DIM]
                up = pltpu.roll(yh, HEAD_DIM - ROPE_DIM // 2, 1)
                dn = pltpu.roll(yh, ROPE_DIM // 2, 1)
                qk_ref[:, c0 + h0:c0 + h0 + HEAD_DIM] = (
                    yh * tab_ref[part, 0] + up * tab_ref[part, 1] + dn * tab_ref[part, 2])
        v_ref[...] = _dot(xn_ref[...], wv_ref[...])

    @pl.when(j == 0)
    def _():
        prologue()
        step()

    @pl.when(j > 0)
    def _():
        step()


def _rope_inv_freq():
    freqs = ROPE_THETA ** (-np.arange(0, ROPE_DIM, 2, dtype=np.float64) / ROPE_DIM)
    return jnp.asarray(freqs.astype(np.float32).reshape(ROPE_DIM // 2, 1))


def _norm_matmul(x, g, w, cast=(), *, tm=1024, tn=PROJ_TILE):
    s, d = x.shape
    n = w.shape[1]
    tm, tn = _pick(s, tm), _pick(n, tn)
    cast_specs = [_cast_specs(cw, lead, s // tm, n // tn) for cw, lead in cast]
    in_specs = [
        pl.BlockSpec((tm, d), lambda i, j: (i, 0)),
        pl.BlockSpec((1, d), lambda i, j: (0, 0)),
        pl.BlockSpec((d, tn), lambda i, j: (0, j)),
    ]
    outs = pl.pallas_call(
        functools.partial(_norm_matmul_kernel, n_cast=len(cast)),
        grid=(s // tm, n // tn),
        in_specs=in_specs + [src for src, _ in cast_specs],
        out_specs=[pl.BlockSpec((tm, tn), lambda i, j: (i, j))] + [dst for _, dst in cast_specs],
        out_shape=[jax.ShapeDtypeStruct((s, n), BF16)]
        + [jax.ShapeDtypeStruct(cw.shape[-2:], BF16) for cw, _ in cast],
        scratch_shapes=[pltpu.VMEM((tm, d), BF16)],
        compiler_params=_params(("parallel", "arbitrary")),
        name="norm_matmul",
    )(x, g.reshape(1, d), w, *[cw for cw, _ in cast])
    return outs[0] if not cast else outs


def _qkv_proj(x, g, w, positions, *, tm=1024, rope_tile=ROPE_TILE):
    s, d = x.shape
    assert w.shape == (d, 3 * d)
    tm, rope_tile = _pick(s, tm), _pick(d, rope_tile)
    n_j = 2 * d // rope_tile
    v_tile = d // n_j
    assert rope_tile % MXU_COLS == 0 and MXU_COLS % HEAD_DIM == 0 and v_tile % HEAD_DIM == 0
    return pl.pallas_call(
        functools.partial(_qkv_proj_kernel, tiles_per_part=d // rope_tile,
                          q_scale=LOG2E / np.sqrt(HEAD_DIM)),
        grid=(s // tm, n_j),
        in_specs=[
            pl.BlockSpec((tm, d), lambda i, j: (i, 0)),
            pl.BlockSpec((1, d), lambda i, j: (0, 0)),
            pl.BlockSpec((d, rope_tile), lambda i, j: (0, j)),
            pl.BlockSpec((d, v_tile), lambda i, j: (0, 2 * d // v_tile + j)),
            pl.BlockSpec((None, 1, tm), lambda i, j: (i, 0, 0)),
            pl.BlockSpec((ROPE_DIM // 2, 1), lambda i, j: (0, 0)),
        ],
        out_specs=[pl.BlockSpec((tm, rope_tile), lambda i, j: (i, j)),
                   pl.BlockSpec((tm, v_tile), lambda i, j: (i, j))],
        out_shape=[jax.ShapeDtypeStruct((s, 2 * d), F32), jax.ShapeDtypeStruct((s, d), F32)],
        scratch_shapes=[pltpu.VMEM((tm, d), BF16), pltpu.VMEM((2, 3, tm, HEAD_DIM), F32)],
        compiler_params=_params(("parallel", "arbitrary")),
        name="qkv_proj",
    )(x, g.reshape(1, d), w, w, positions.reshape(s // tm, 1, tm), _rope_inv_freq())


def _proj_residual_kernel(a_ref, h_ref, w_ref, o_ref):
    o_ref[...] = h_ref[...] + _dot(a_ref[...], w_ref[...])


def _proj_residual(a, h, w, *, tm=1024):
    s, d = h.shape
    tm = _pick(s, tm)
    row = lambda width: pl.BlockSpec((tm, width), lambda i: (i, 0))
    return pl.pallas_call(
        _proj_residual_kernel,
        grid=(s // tm,),
        in_specs=[row(a.shape[1]), row(d), pl.BlockSpec(w.shape, lambda i: (0, 0))],
        out_specs=row(d),
        out_shape=jax.ShapeDtypeStruct((s, d), F32),
        compiler_params=_params(("parallel",)),
        name="proj_residual",
    )(a, h, w)


def _mixer_kernel(zc_ref, zh_ref, h_ref, cw_ref, pw_ref, ps_ref, wo_ref, o_ref, y_ref, *, tm, cdim, pgroup):
    i = pl.program_id(0)
    has_prev = i > 0
    chunk = pgroup

    def ext(col0):
        sl = slice(col0, col0 + chunk)
        halo = jnp.where(has_prev, zh_ref[:, sl].astype(F32), 0.0)
        return jnp.concatenate([halo, zc_ref[:, sl].astype(F32)], axis=0)

    for c0 in range(0, cdim, chunk):
        cx = ext(cdim + c0) * ext(2 * cdim + c0)
        w = cw_ref[:, c0:c0 + chunk]
        conv = (w[2:3] * cx + w[1:2] * pltpu.roll(cx, 1, 0) + w[0:1] * pltpu.roll(cx, 2, 0))[HALO:]
        gate_b = zc_ref[:, c0:c0 + chunk].astype(F32)
        y_ref[:, c0:c0 + chunk] = (gate_b * conv).astype(BF16)

    t = i * tm + lax.broadcasted_iota(jnp.int32, (tm, 1), 0)
    for gi, win in enumerate(POOL_WINDOWS):
        col0 = 3 * cdim + gi * chunk
        acc = ext(col0)
        sh = 1
        while sh < win:
            acc = acc + pltpu.roll(acc, sh, 0)
            sh *= 2
        inv_cnt = 1.0 / jnp.minimum(t + 1, win).astype(F32)
        u = zc_ref[:, col0:col0 + chunk].astype(F32)
        pooled = acc[HALO:] * inv_cnt - u
        mixed = _dot(pooled.astype(BF16), pw_ref[gi])
        y_ref[:, cdim + gi * chunk: cdim + (gi + 1) * chunk] = (
            mixed * ps_ref[:, gi * chunk:(gi + 1) * chunk]).astype(BF16)

    o_ref[...] = h_ref[...] + _dot(y_ref[...], wo_ref[...])


def _mixer(z, h, conv_w, pool_w, pool_scale, w_out, *, tm=512):
    s, d = h.shape
    cdim = d // 2
    pgroup = pool_w.shape[-1]
    tm = _pick(s, tm)
    assert tm % HALO == 0 and cdim % pgroup == 0 and max(POOL_WINDOWS) <= HALO
    hb = tm // HALO
    return pl.pallas_call(
        functools.partial(_mixer_kernel, tm=tm, cdim=cdim, pgroup=pgroup),
        grid=(s // tm,),
        in_specs=[
            pl.BlockSpec((tm, 4 * cdim), lambda i: (i, 0)),
            pl.BlockSpec((HALO, 4 * cdim), lambda i: (jnp.maximum(i * hb - 1, 0), 0)),
            pl.BlockSpec((tm, d), lambda i: (i, 0)),
            pl.BlockSpec((CONV_WIDTH, cdim), lambda i: (0, 0)),
            pl.BlockSpec(pool_w.shape, lambda i: (0, 0, 0)),
            pl.BlockSpec((1, cdim), lambda i: (0, 0)),
            pl.BlockSpec((d, d), lambda i: (0, 0)),
        ],
        out_specs=pl.BlockSpec((tm, d), lambda i: (i, 0)),
        out_shape=jax.ShapeDtypeStruct((s, d), F32),
        scratch_shapes=[pltpu.VMEM((tm, d), BF16)],
        compiler_params=_params(("parallel",)),
        name="mixer",
    )(z, z, h, conv_w, pool_w, pool_scale.reshape(1, cdim), w_out)


def _attn_kernel(q_ref, k_ref, v_ref, o_ref, *scr, dils, span):
    n_br = len(dils)
    ks, vs, qs, ogs, lgs = (scr[i * n_br:(i + 1) * n_br] for i in range(5))
    t = pl.program_id(1)
    blk = ATT_BLOCK
    tile = q_ref.shape[0]

    @pl.when(t == 0)
    def _():
        for kd, vd in zip(ks, vs):
            kd[:, :blk, :] = jnp.zeros((kd.shape[0], blk, HEAD_DIM), BF16)
            vd[:, :blk, :] = jnp.zeros((vd.shape[0], blk, HEAD_DIM), BF16)

    @pl.when(t > 0)
    def _():
        for kd, vd, d in zip(ks, vs, dils):
            n = tile // d
            kd[:, :blk, :] = kd[:, n:n + blk, :]
            vd[:, :blk, :] = vd[:, n:n + blk, :]

    for kd, vd, qd, d in zip(ks, vs, qs, dils):
        n = tile // d
        for r in range(d):
            rows = pl.ds(r, n, stride=d) if d > 1 else slice(None)
            kd[r, blk:, :] = k_ref[rows, :].astype(BF16)
            vd[r, blk:, :] = v_ref[rows, :].astype(BF16)
            qd[r] = q_ref[rows, :].astype(BF16)

    lm = lax.broadcasted_iota(jnp.int32, (8, blk), 1) - lax.broadcasted_iota(jnp.int32, (8, blk), 0)
    never = jnp.where(t == 0, 2 * blk, 0)

    def masked(s, first):
        slabs = []
        for j8 in range(0, blk, 8):
            lo = j8 + blk - span
            prev = jnp.where(lm >= (lo + never if first else lo), s[j8:j8 + 8, :blk], NEG_INF)
            cur = jnp.where(lm <= j8, s[j8:j8 + 8, blk:], NEG_INF)
            slabs.append(jnp.concatenate([prev, cur], axis=1))
        return jnp.concatenate(slabs, axis=0)

    for g, d in enumerate(dils):
        n = tile // d
        for r in range(d):
            for b in range(n // blk):
                q = qs[g][r, b * blk:(b + 1) * blk, :]
                kk = ks[g][r, b * blk:(b + 2) * blk, :]
                vv = vs[g][r, b * blk:(b + 2) * blk, :]
                s = lax.dot_general(q, kk, (((1,), (1,)), ((), ())), preferred_element_type=F32)
                s = masked(s, b == 0)
                m = jnp.max(s, axis=-1, keepdims=True)
                p = jnp.exp2(s - m)
                l = jnp.sum(p, axis=-1, keepdims=True)
                o = _dot(p.astype(BF16), vv) / l
                lse = m * LN2 + jnp.log(l)
                rows = pl.ds(b * blk * d + r, blk, stride=d) if d > 1 else slice(b * blk, (b + 1) * blk)
                ogs[g][rows, :] = o
                lgs[g][rows, :] = jnp.broadcast_to(lse, (blk, HEAD_DIM))

    for c0 in range(0, tile, COMBINE_ROWS):
        rows = slice(c0, c0 + COMBINE_ROWS)
        ls = [lg[rows, :] for lg in lgs]
        m = functools.reduce(jnp.maximum, ls)
        es = [jnp.exp(x - m) for x in ls]
        tot = functools.reduce(lambda x, y: x + y, es)
        acc = functools.reduce(lambda x, y: x + y, [e * og[rows, :] for e, og in zip(es, ogs)])
        o_ref[rows, :] = (acc / tot).astype(o_ref.dtype)


def _attention(qk, v, *, tile=ATT_TILE):
    s, d_model = v.shape
    n_heads = d_model // HEAD_DIM
    dils = tuple(d for _, d in DIL_BRANCHES)
    spans = {w // d for w, d in DIL_BRANCHES}
    assert len(spans) == 1 and s % tile == 0 and all(tile % (ATT_BLOCK * d) == 0 for d in dils)
    span = spans.pop()
    assert ATT_BLOCK - 1 <= span <= ATT_BLOCK
    kv = [pltpu.VMEM((d, ATT_BLOCK + tile // d, HEAD_DIM), BF16) for d in dils]
    qd = [pltpu.VMEM((d, tile // d, HEAD_DIM), BF16) for d in dils]
    acc = [pltpu.VMEM((tile, HEAD_DIM), F32) for _ in dils]
    blockspec = lambda part: pl.BlockSpec((tile, HEAD_DIM), lambda h, t: (t, part * n_heads + h))
    return pl.pallas_call(
        functools.partial(_attn_kernel, dils=dils, span=span),
        grid=(n_heads, s // tile),
        in_specs=[blockspec(0), blockspec(1), blockspec(0)],
        out_specs=pl.BlockSpec((tile, HEAD_DIM), lambda h, t: (t, h)),
        out_shape=jax.ShapeDtypeStruct((s, d_model), BF16),
        scratch_shapes=kv + kv + qd + acc + acc,
        compiler_params=_params(("parallel", "arbitrary")),
        name="attn",
    )(qk, qk, v)


def kernel(x, positions, norm_g, ffn_w1, ffn_w3, ffn_w2, mix_w_in, conv_w, pool_w, pool_scale,
           mix_w_out, attn_w_qkv, attn_w_o, final_g):
    b, s, d = x.shape
    depth = norm_g.shape[0]
    bf = lambda w: w.astype(BF16)
    ffns = [(i, j) for i in range(depth) for j in range(2)]
    wts = {ffns[0]: [bf(w[ffns[0]]) for w in (ffn_w1, ffn_w3, ffn_w2)]}

    def ffn(h, key, g, final_g=None):
        nxt = ffns.index(key) + 1
        if nxt < len(ffns) and ffns[nxt] not in wts:
            h, *wts[ffns[nxt]] = _ffn(h, g, *wts[key], final_g=final_g,
                                      cast=[(w, ffns[nxt]) for w in (ffn_w1, ffn_w3, ffn_w2)])
            return h
        return _ffn(h, g, *wts[key], final_g=final_g)

    proj_w = {}
    outs = []
    for bi in range(b):
        h = x[bi]
        for i in range(depth):
            h = ffn(h, (i, 0), norm_g[i, 0])
            j = i // 2
            if i % 2 == 0:
                later = {"mix_out": (mix_w_out, (j,))}
                if i + 1 < depth:
                    later.update(qkv=(attn_w_qkv, (j,)), attn_out=(attn_w_o, (j,)))
                later = {k: v for k, v in later.items() if (k, v[1]) not in proj_w}
                res = _norm_matmul(h, norm_g[i, 1], bf(mix_w_in[j]), cast=list(later.values()))
                z, cast_w = (res[0], res[1:]) if later else (res, [])
                proj_w.update({(k, v[1]): cw for (k, v), cw in zip(later.items(), cast_w)})
                h = _mixer(z, h, conv_w[j], bf(pool_w[j]), pool_scale[j], proj_w["mix_out", (j,)])
            else:
                w_qkv = proj_w.get(("qkv", (j,)))
                w_o = proj_w.get(("attn_out", (j,)))
                w_qkv = bf(attn_w_qkv[j]) if w_qkv is None else w_qkv
                w_o = bf(attn_w_o[j]) if w_o is None else w_o
                qk, v = _qkv_proj(h, norm_g[i, 1], w_qkv, positions[bi])
                h = _proj_residual(_attention(qk, v), h, w_o)
            h = ffn(h, (i, 1), norm_g[i, 2], final_g=final_g if i == depth - 1 else None)
        outs.append(h.reshape(1, s, d))
    return outs[0] if b == 1 else jnp.concatenate(outs, axis=0)
```

```python
import functools

import numpy as np
import jax
import jax.numpy as jnp
from jax import lax
from jax.experimental import pallas as pl
from jax.experimental.pallas import tpu as pltpu

F32 = jnp.float32
BF16 = jnp.bfloat16

HEAD_DIM = 128
ROPE_DIM = HEAD_DIM // 4
ROPE_THETA = 500000.0
DIL_BRANCHES = ((128, 1), (512, 4), (2048, 16))
ATT_BLOCK = 128
ATT_TILE = 2 * ATT_BLOCK * max(d for _, d in DIL_BRANCHES)
COMBINE_ROWS = 256
EPILOGUE_ROWS = 256
CONV_WIDTH = 3
POOL_WINDOWS = (2, 4, 8, 16)
HALO = 16
RMS_EPS = 1e-6
NEG_INF = -1e30
LOG2E = float(np.log2(np.e))
LN2 = float(np.log(2.0))

MXU_COLS = 256
ROPE_TILE = 1024
PROJ_TILE = 2048
V7X_VMEM_BYTES = 64 * 1024 * 1024
VMEM_LIMIT = V7X_VMEM_BYTES - 6 * 1024 * 1024


def _params(sem):
    return pltpu.CompilerParams(dimension_semantics=sem, vmem_limit_bytes=VMEM_LIMIT)


def _rms(x, g):
    ms = jnp.mean(x * x, axis=-1, keepdims=True)
    return x * lax.rsqrt(ms + RMS_EPS) * g


def _dot(a, b):
    return jnp.dot(a, b, preferred_element_type=F32)


def _pick(n, pref):
    t = min(n, pref)
    while n % t:
        t //= 2
    return t


def _ffn_kernel(x_ref, g_ref, w1_ref, w3_ref, w2_ref, *rest, n_f, final_norm, n_cast):
    rest = list(rest)
    fg_ref = rest.pop(0) if final_norm else None
    cast_in = [rest.pop(0) for _ in range(n_cast)]
    o_ref = rest.pop(0)
    cast_out = [rest.pop(0) for _ in range(n_cast)]
    (xn_ref,) = rest
    f = pl.program_id(1)

    def step():
        xn = xn_ref[...]
        h1 = _dot(xn, w1_ref[...])
        h3 = _dot(xn, w3_ref[...])
        a = (h1 * jax.nn.sigmoid(h1) * (0.5 * h3)).astype(BF16)
        o_ref[...] += _dot(a, w2_ref[...])
        for src, dst in zip(cast_in, cast_out):
            dst[...] = src[...].astype(BF16)

    @pl.when(f == 0)
    def _():
        xn_ref[...] = _rms(x_ref[...], g_ref[...]).astype(BF16)
        o_ref[...] = x_ref[...]
        step()

    @pl.when(f > 0)
    def _():
        step()

    if final_norm:
        @pl.when(f == n_f - 1)
        def _():
            def chunk(c, carry):
                rows = pl.ds(pl.multiple_of(c * EPILOGUE_ROWS, EPILOGUE_ROWS), EPILOGUE_ROWS)
                o_ref[rows, :] = _rms(o_ref[rows, :], fg_ref[...])
                return carry

            lax.fori_loop(0, o_ref.shape[0] // EPILOGUE_ROWS, chunk, 0)


def _cast_specs(w, lead, n_i, n_f):
    r, c = w.shape[-2:]
    wide = c >= r
    shape = (r // n_i, c // n_f) if wide else (r // n_f, c // n_i)
    assert shape[0] * (n_i if wide else n_f) == r and shape[1] * (n_f if wide else n_i) == c
    assert shape[0] % 16 == 0 and shape[1] % 128 == 0, shape
    block = (lambda i, f: (i, f)) if wide else (lambda i, f: (f, i))
    src = pl.BlockSpec((None,) * len(lead) + shape, lambda i, f: tuple(lead) + block(i, f))
    return src, pl.BlockSpec(shape, block)


def _ffn(x, g, w1, w3, w2, final_g=None, cast=(), *, tm=1024, tf=512):
    s, d = x.shape
    ff = w1.shape[-1]
    tm, tf = _pick(s, tm), _pick(ff, tf)
    n_i, n_f = s // tm, ff // tf
    final_norm = final_g is not None
    in_specs = [
        pl.BlockSpec((tm, d), lambda i, f: (i, 0)),
        pl.BlockSpec((1, d), lambda i, f: (0, 0)),
        pl.BlockSpec((d, tf), lambda i, f: (0, f)),
        pl.BlockSpec((d, tf), lambda i, f: (0, f)),
        pl.BlockSpec((tf, d), lambda i, f: (f, 0)),
    ]
    args = [x, g.reshape(1, d), w1, w3, w2]
    if final_norm:
        in_specs.append(pl.BlockSpec((1, d), lambda i, f: (0, 0)))
        args.append(final_g.reshape(1, d))
    cast_specs = [_cast_specs(w, lead, n_i, n_f) for w, lead in cast]
    outs = pl.pallas_call(
        functools.partial(_ffn_kernel, n_f=n_f, final_norm=final_norm, n_cast=len(cast)),
        grid=(n_i, n_f),
        in_specs=in_specs + [src for src, _ in cast_specs],
        out_specs=[pl.BlockSpec((tm, d), lambda i, f: (i, 0))] + [dst for _, dst in cast_specs],
        out_shape=[jax.ShapeDtypeStruct((s, d), F32)]
        + [jax.ShapeDtypeStruct(w.shape[-2:], BF16) for w, _ in cast],
        scratch_shapes=[pltpu.VMEM((tm, d), BF16)],
        compiler_params=_params(("parallel", "arbitrary")),
        name="ffn",
    )(*args, *[w for w, _ in cast])
    return outs[0] if not cast else outs


def _norm_matmul_kernel(x_ref, g_ref, w_ref, *rest, n_cast):
    cast_in, (o_ref, *cast_out), xn_ref = rest[:n_cast], rest[n_cast:2 * n_cast + 1], rest[-1]
    j = pl.program_id(1)

    def step():
        for src, dst in zip(cast_in, cast_out):
            dst[...] = src[...].astype(BF16)
        o_ref[...] = _dot(xn_ref[...], w_ref[...]).astype(o_ref.dtype)

    @pl.when(j == 0)
    def _():
        xn_ref[...] = _rms(x_ref[...], g_ref[...]).astype(BF16)
        step()

    @pl.when(j > 0)
    def _():
        step()


def _qkv_proj_kernel(x_ref, g_ref, wr_ref, wv_ref, pos_ref, inv_ref, qk_ref, v_ref, xn_ref, tab_ref,
                     *, tiles_per_part, q_scale):
    j = pl.program_id(1)

    def prologue():
        xn_ref[...] = _rms(x_ref[...], g_ref[...]).astype(BF16)
        half = ROPE_DIM // 2
        ang = inv_ref[...] * pos_ref[...].astype(F32)
        cs, sn = jnp.cos(ang), jnp.sin(ang)
        zeros = lambda n: jnp.zeros((n, ang.shape[1]), F32)
        c = jnp.concatenate([cs, cs, jnp.ones((HEAD_DIM - 2 * half, ang.shape[1]), F32)], axis=0).T
        a = jnp.concatenate([-sn, zeros(HEAD_DIM - half)], axis=0).T
        b = jnp.concatenate([zeros(half), sn, zeros(HEAD_DIM - 2 * half)], axis=0).T
        for idx, tab in enumerate((c, a, b)):
            tab_ref[0, idx] = tab * q_scale
            tab_ref[1, idx] = tab

    def step():
        part = j // tiles_per_part
        for c0 in range(0, wr_ref.shape[1], MXU_COLS):
            y = _dot(xn_ref[...], wr_ref[:, c0:c0 + MXU_COLS])
            for h0 in range(0, MXU_COLS, HEAD_DIM):
                yh = y[:, h0:h0 + HEAD_DIM]
                up = pltpu.roll(yh, HEAD_DIM - ROPE_DIM // 2, 1)
                dn = pltpu.roll(yh, ROPE_DIM // 2, 1)
                qk_ref[:, c0 + h0:c0 + h0 + HEAD_DIM] = (
                    yh * tab_ref[part, 0] + up * tab_ref[part, 1] + dn * tab_ref[part, 2])
        v_ref[...] = _dot(xn_ref[...], wv_ref[...])

    @pl.when(j == 0)
    def _():
        prologue()
        step()

    @pl.when(j > 0)
    def _():
        step()


def _rope_inv_freq():
    freqs = ROPE_THETA ** (-np.arange(0, ROPE_DIM, 2, dtype=np.float64) / ROPE_DIM)
    return jnp.asarray(freqs.astype(np.float32).reshape(ROPE_DIM // 2, 1))


def _norm_matmul(x, g, w, cast=(), *, tm=1024, tn=PROJ_TILE):
    s, d = x.shape
    n = w.shape[1]
    tm, tn = _pick(s, tm), _pick(n, tn)
    cast_specs = [_cast_specs(cw, lead, s // tm, n // tn) for cw, lead in cast]
    in_specs = [
        pl.BlockSpec((tm, d), lambda i, j: (i, 0)),
        pl.BlockSpec((1, d), lambda i, j: (0, 0)),
        pl.BlockSpec((d, tn), lambda i, j: (0, j)),
    ]
    outs = pl.pallas_call(
        functools.partial(_norm_matmul_kernel, n_cast=len(cast)),
        grid=(s // tm, n // tn),
        in_specs=in_specs + [src for src, _ in cast_specs],
        out_specs=[pl.BlockSpec((tm, tn), lambda i, j: (i, j))] + [dst for _, dst in cast_specs],
        out_shape=[jax.ShapeDtypeStruct((s, n), BF16)]
        + [jax.ShapeDtypeStruct(cw.shape[-2:], BF16) for cw, _ in cast],
        scratch_shapes=[pltpu.VMEM((tm, d), BF16)],
        compiler_params=_params(("parallel", "arbitrary")),
        name="norm_matmul",
    )(x, g.reshape(1, d), w, *[cw for cw, _ in cast])
    return outs[0] if not cast else outs


def _qkv_proj(x, g, w, positions, *, tm=1024, rope_tile=ROPE_TILE):
    s, d = x.shape
    assert w.shape == (d, 3 * d)
    tm, rope_tile = _pick(s, tm), _pick(d, rope_tile)
    n_j = 2 * d // rope_tile
    v_tile = d // n_j
    assert rope_tile % MXU_COLS == 0 and MXU_COLS % HEAD_DIM == 0 and v_tile % HEAD_DIM == 0
    return pl.pallas_call(
        functools.partial(_qkv_proj_kernel, tiles_per_part=d // rope_tile,
                          q_scale=LOG2E / np.sqrt(HEAD_DIM)),
        grid=(s // tm, n_j),
        in_specs=[
            pl.BlockSpec((tm, d), lambda i, j: (i, 0)),
            pl.BlockSpec((1, d), lambda i, j: (0, 0)),
            pl.BlockSpec((d, rope_tile), lambda i, j: (0, j)),
            pl.BlockSpec((d, v_tile), lambda i, j: (0, 2 * d // v_tile + j)),
            pl.BlockSpec((None, 1, tm), lambda i, j: (i, 0, 0)),
            pl.BlockSpec((ROPE_DIM // 2, 1), lambda i, j: (0, 0)),
        ],
        out_specs=[pl.BlockSpec((tm, rope_tile), lambda i, j: (i, j)),
                   pl.BlockSpec((tm, v_tile), lambda i, j: (i, j))],
        out_shape=[jax.ShapeDtypeStruct((s, 2 * d), F32), jax.ShapeDtypeStruct((s, d), F32)],
        scratch_shapes=[pltpu.VMEM((tm, d), BF16), pltpu.VMEM((2, 3, tm, HEAD_DIM), F32)],
        compiler_params=_params(("parallel", "arbitrary")),
        name="qkv_proj",
    )(x, g.reshape(1, d), w, w, positions.reshape(s // tm, 1, tm), _rope_inv_freq())


def _proj_residual_kernel(a_ref, h_ref, w_ref, o_ref):
    o_ref[...] = h_ref[...] + _dot(a_ref[...], w_ref[...])


def _proj_residual(a, h, w, *, tm=512):
    s, d = h.shape
    tm = _pick(s, tm)
    row = lambda width: pl.BlockSpec((tm, width), lambda i: (i, 0))
    return pl.pallas_call(
        _proj_residual_kernel,
        grid=(s // tm,),
        in_specs=[row(a.shape[1]), row(d), pl.BlockSpec(w.shape, lambda i: (0, 0))],
        out_specs=row(d),
        out_shape=jax.ShapeDtypeStruct((s, d), F32),
        compiler_params=_params(("parallel",)),
        name="proj_residual",
    )(a, h, w)


def _mixer_kernel(zc_ref, zh_ref, h_ref, cw_ref, pw_ref, ps_ref, wo_ref, o_ref, y_ref, *, tm, cdim, pgroup):
    i = pl.program_id(0)
    has_prev = i > 0
    chunk = pgroup

    def ext(col0):
        sl = slice(col0, col0 + chunk)
        halo = jnp.where(has_prev, zh_ref[:, sl].astype(F32), 0.0)
        return jnp.concatenate([halo, zc_ref[:, sl].astype(F32)], axis=0)

    for c0 in range(0, cdim, chunk):
        cx = ext(cdim + c0) * ext(2 * cdim + c0)
        w = cw_ref[:, c0:c0 + chunk]
        conv = (w[2:3] * cx + w[1:2] * pltpu.roll(cx, 1, 0) + w[0:1] * pltpu.roll(cx, 2, 0))[HALO:]
        gate_b = zc_ref[:, c0:c0 + chunk].astype(F32)
        y_ref[:, c0:c0 + chunk] = (gate_b * conv).astype(BF16)

    t = i * tm + lax.broadcasted_iota(jnp.int32, (tm, 1), 0)
    ncol = o_ref.shape[1] // len(POOL_WINDOWS)
    for gi, win in enumerate(POOL_WINDOWS):
        oc = slice(gi * ncol, (gi + 1) * ncol)
        o_ref[:, oc] = h_ref[:, oc] + _dot(y_ref[:, :cdim], wo_ref[:cdim, oc])
        col0 = 3 * cdim + gi * chunk
        acc = ext(col0)
        sh = 1
        while sh < win:
            acc = acc + pltpu.roll(acc, sh, 0)
            sh *= 2
        inv_cnt = 1.0 / jnp.minimum(t + 1, win).astype(F32)
        u = zc_ref[:, col0:col0 + chunk].astype(F32)
        pooled = acc[HALO:] * inv_cnt - u
        mixed = _dot(pooled.astype(BF16), pw_ref[gi])
        y_ref[:, cdim + gi * chunk: cdim + (gi + 1) * chunk] = (
            mixed * ps_ref[:, gi * chunk:(gi + 1) * chunk]).astype(BF16)

    for gi in range(len(POOL_WINDOWS)):
        oc = slice(gi * ncol, (gi + 1) * ncol)
        o_ref[:, oc] += _dot(y_ref[:, cdim:], wo_ref[cdim:, oc])


def _mixer(z, h, conv_w, pool_w, pool_scale, w_out, *, tm=512):
    s, d = h.shape
    cdim = d // 2
    pgroup = pool_w.shape[-1]
    tm = _pick(s, tm)
    assert tm % HALO == 0 and cdim % pgroup == 0 and max(POOL_WINDOWS) <= HALO
    hb = tm // HALO
    return pl.pallas_call(
        functools.partial(_mixer_kernel, tm=tm, cdim=cdim, pgroup=pgroup),
        grid=(s // tm,),
        in_specs=[
            pl.BlockSpec((tm, 4 * cdim), lambda i: (i, 0)),
            pl.BlockSpec((HALO, 4 * cdim), lambda i: (jnp.maximum(i * hb - 1, 0), 0)),
            pl.BlockSpec((tm, d), lambda i: (i, 0)),
            pl.BlockSpec((CONV_WIDTH, cdim), lambda i: (0, 0)),
            pl.BlockSpec(pool_w.shape, lambda i: (0, 0, 0)),
            pl.BlockSpec((1, cdim), lambda i: (0, 0)),
            pl.BlockSpec((d, d), lambda i: (0, 0)),
        ],
        out_specs=pl.BlockSpec((tm, d), lambda i: (i, 0)),
        out_shape=jax.ShapeDtypeStruct((s, d), F32),
        scratch_shapes=[pltpu.VMEM((tm, d), BF16)],
        compiler_params=_params(("parallel",)),
        name="mixer",
    )(z, z, h, conv_w, pool_w, pool_scale.reshape(1, cdim), w_out)


def _attn_kernel(q_ref, k_ref, v_ref, o_ref, *scr, dils, span):
    n_br = len(dils)
    ks, vs, qs, ogs, lgs = (scr[i * n_br:(i + 1) * n_br] for i in range(5))
    t = pl.program_id(1)
    blk = ATT_BLOCK
    tile = q_ref.shape[0]

    @pl.when(t == 0)
    def _():
        for kd, vd in zip(ks, vs):
            kd[:, :blk, :] = jnp.zeros((kd.shape[0], blk, HEAD_DIM), BF16)
            vd[:, :blk, :] = jnp.zeros((vd.shape[0], blk, HEAD_DIM), BF16)

    @pl.when(t > 0)
    def _():
        for kd, vd, d in zip(ks, vs, dils):
            n = tile // d
            kd[:, :blk, :] = kd[:, n:n + blk, :]
            vd[:, :blk, :] = vd[:, n:n + blk, :]

    for kd, vd, qd, d in zip(ks, vs, qs, dils):
        n = tile // d
        for r in range(d):
            rows = pl.ds(r, n, stride=d) if d > 1 else slice(None)
            kd[r, blk:, :] = k_ref[rows, :].astype(BF16)
            vd[r, blk:, :] = v_ref[rows, :].astype(BF16)
            qd[r] = q_ref[rows, :].astype(BF16)

    lm = lax.broadcasted_iota(jnp.int32, (8, blk), 1) - lax.broadcasted_iota(jnp.int32, (8, blk), 0)
    never = jnp.where(t == 0, 2 * blk, 0)

    def masked(s, first):
        slabs = []
        for j8 in range(0, blk, 8):
            lo = j8 + blk - span
            prev = jnp.where(lm >= (lo + never if first else lo), s[j8:j8 + 8, :blk], NEG_INF)
            cur = jnp.where(lm <= j8, s[j8:j8 + 8, blk:], NEG_INF)
            slabs.append(jnp.concatenate([prev, cur], axis=1))
        return jnp.concatenate(slabs, axis=0)

    for g, d in enumerate(dils):
        n = tile // d
        for r in range(d):
            for b in range(n // blk):
                q = qs[g][r, b * blk:(b + 1) * blk, :]
                kk = ks[g][r, b * blk:(b + 2) * blk, :]
                vv = vs[g][r, b * blk:(b + 2) * blk, :]
                s = lax.dot_general(q, kk, (((1,), (1,)), ((), ())), preferred_element_type=F32)
                s = masked(s, b == 0)
                m = jnp.max(s, axis=-1, keepdims=True)
                p = jnp.exp2(s - m)
                l = jnp.sum(p, axis=-1, keepdims=True)
                o = _dot(p.astype(BF16), vv) / l
                lse = m * LN2 + jnp.log(l)
                rows = pl.ds(b * blk * d + r, blk, stride=d) if d > 1 else slice(b * blk, (b + 1) * blk)
                ogs[g][rows, :] = o
                lgs[g][rows, :] = jnp.broadcast_to(lse, (blk, HEAD_DIM))

    for c0 in range(0, tile, COMBINE_ROWS):
        rows = slice(c0, c0 + COMBINE_ROWS)
        ls = [lg[rows, :] for lg in lgs]
        m = functools.reduce(jnp.maximum, ls)
        es = [jnp.exp(x - m) for x in ls]
        tot = functools.reduce(lambda x, y: x + y, es)
        acc = functools.reduce(lambda x, y: x + y, [e * og[rows, :] for e, og in zip(es, ogs)])
        o_ref[rows, :] = (acc / tot).astype(o_ref.dtype)


def _attention(qk, v, *, tile=ATT_TILE):
    s, d_model = v.shape
    n_heads = d_model // HEAD_DIM
    dils = tuple(d for _, d in DIL_BRANCHES)
    spans = {w // d for w, d in DIL_BRANCHES}
    assert len(spans) == 1 and s % tile == 0 and all(tile % (ATT_BLOCK * d) == 0 for d in dils)
    span = spans.pop()
    assert ATT_BLOCK - 1 <= span <= ATT_BLOCK
    kv = [pltpu.VMEM((d, ATT_BLOCK + tile // d, HEAD_DIM), BF16) for d in dils]
    qd = [pltpu.VMEM((d, tile // d, HEAD_DIM), BF16) for d in dils]
    acc = [pltpu.VMEM((tile, HEAD_DIM), F32) for _ in dils]
    blockspec = lambda part: pl.BlockSpec((tile, HEAD_DIM), lambda h, t: (t, part * n_heads + h))
    return pl.pallas_call(
        functools.partial(_attn_kernel, dils=dils, span=span),
        grid=(n_heads, s // tile),
        in_specs=[blockspec(0), blockspec(1), blockspec(0)],
        out_specs=pl.BlockSpec((tile, HEAD_DIM), lambda h, t: (t, h)),
        out_shape=jax.ShapeDtypeStruct((s, d_model), BF16),
        scratch_shapes=kv + kv + qd + acc + acc,
        compiler_params=_params(("parallel", "arbitrary")),
        name="attn",
    )(qk, qk, v)


def kernel(x, positions, norm_g, ffn_w1, ffn_w3, ffn_w2, mix_w_in, conv_w, pool_w, pool_scale,
           mix_w_out, attn_w_qkv, attn_w_o, final_g):
    b, s, d = x.shape
    depth = norm_g.shape[0]
    bf = lambda w: w.astype(BF16)
    ffns = [(i, j) for i in range(depth) for j in range(2)]
    wts = {ffns[0]: [bf(w[ffns[0]]) for w in (ffn_w1, ffn_w3, ffn_w2)]}

    def ffn(h, key, g, final_g=None):
        nxt = ffns.index(key) + 1
        if nxt < len(ffns) and ffns[nxt] not in wts:
            h, *wts[ffns[nxt]] = _ffn(h, g, *wts[key], final_g=final_g,
                                      cast=[(w, ffns[nxt]) for w in (ffn_w1, ffn_w3, ffn_w2)])
            return h
        return _ffn(h, g, *wts[key], final_g=final_g)

    proj_w = {}
    outs = []
    for bi in range(b):
        h = x[bi]
        for i in range(depth):
            h = ffn(h, (i, 0), norm_g[i, 0])
            j = i // 2
            if i % 2 == 0:
                later = {"mix_out": (mix_w_out, (j,))}
                if i + 1 < depth:
                    later.update(qkv=(attn_w_qkv, (j,)), attn_out=(attn_w_o, (j,)))
                later = {k: v for k, v in later.items() if (k, v[1]) not in proj_w}
                res = _norm_matmul(h, norm_g[i, 1], bf(mix_w_in[j]), cast=list(later.values()))
                z, cast_w = (res[0], res[1:]) if later else (res, [])
                proj_w.update({(k, v[1]): cw for (k, v), cw in zip(later.items(), cast_w)})
                h = _mixer(z, h, conv_w[j], bf(pool_w[j]), pool_scale[j], proj_w["mix_out", (j,)])
            else:
                w_qkv = proj_w.get(("qkv", (j,)))
                w_o = proj_w.get(("attn_out", (j,)))
                w_qkv = bf(attn_w_qkv[j]) if w_qkv is None else w_qkv
                w_o = bf(attn_w_o[j]) if w_o is None else w_o
                qk, v = _qkv_proj(h, norm_g[i, 1], w_qkv, positions[bi])
                h = _proj_residual(_attention(qk, v), h, w_o)
            h = ffn(h, (i, 1), norm_g[i, 2], final_g=final_g if i == depth - 1 else None)
        outs.append(h.reshape(1, s, d))
    return outs[0] if b == 1 else jnp.concatenate(outs, axis=0)
```
